```python
import math
import jax
import jax.numpy as jnp
from jax import lax
import numpy as np

D_MODEL = 1024
BATCH = 16
SEQ = 2048
DEPTH = 2
DEC_BATCH = 128
DEC_SEQ = 4
PAST_LEN = 16384
PAGE_SIZE = 128

S5_WIDTH = D_MODEL // 4
S5_GROUP = 16
S5_GROUPS = S5_WIDTH // S5_GROUP
S5_P = 64
ML_WIDTH = D_MODEL // 4
ML_DK = 64
ML_DV = 64
ML_HEADS = ML_WIDTH // ML_DV
ML_QK = ML_HEADS * ML_DK
ML_CONV = 4
ML_CHUNK = 64
MLA_WIDTH = D_MODEL // 2
MLA_V = 64
MLA_HEADS = MLA_WIDTH // MLA_V
MLA_NOPE = 64
MLA_ROPE = 32
MLA_Q_LORA = D_MODEL // 4
MLA_KV_LORA = D_MODEL // 8
MLA_SCALE = 1.0 / math.sqrt(MLA_NOPE + MLA_ROPE)
ROPE_THETA = 10000.0
Q_BLOCK = 128
MIX_WIDTH = S5_WIDTH + ML_WIDTH + MLA_WIDTH
N_MEM = 256
CA_HEADS = 4
CA_HD = 64
CA_WIDTH = CA_HEADS * CA_HD
D_FF = 4 * D_MODEL
EPS = 1e-6
IN_SIZES = (S5_WIDTH, ML_QK, ML_QK, ML_WIDTH, ML_HEADS, ML_HEADS, ML_WIDTH, MLA_Q_LORA, MLA_KV_LORA, MLA_ROPE)
N_IN = sum(IN_SIZES)

kernel_name = 'hymba_s5_mlstm_mla_decode_step'


def _rmsnorm(x, g):
    xf = x.astype(jnp.float32)
    y = xf * lax.rsqrt(jnp.mean(xf * xf, axis=-1, keepdims=True) + EPS)
    return (y * g.astype(jnp.float32)).astype(x.dtype)


def _split(z, sizes):
    bounds = [int(v) for v in np.cumsum(sizes)[:-1]]
    return jnp.split(z, bounds, axis=-1)


def _rope(x, pos):
    r = x.shape[-1]
    half = r // 2
    inv = ROPE_THETA ** (-jnp.arange(half, dtype=jnp.float32) * 2.0 / r)
    ang = pos.astype(jnp.float32)[:, None] * inv[None, :]
    if x.ndim == 4:
        ang = ang[:, None, :]
    cos, sin = jnp.cos(ang), jnp.sin(ang)
    xf = x.astype(jnp.float32)
    x1, x2 = xf[..., :half], xf[..., half:]
    return jnp.concatenate([x1 * cos - x2 * sin, x1 * sin + x2 * cos], axis=-1).astype(x.dtype)


def _complex_affine_combine(e1, e2):
    a1r, a1i, b1r, b1i = e1
    a2r, a2i, b2r, b2i = e2
    return (a2r * a1r - a2i * a1i, a2r * a1i + a2i * a1r,
            a2r * b1r - a2i * b1i + b2r, a2r * b1i + a2i * b1r + b2i)


def _s5_mixer(u, h0, A_re, A_im, log_dt, B_re, B_im, C_re, C_im, d_skip, w_glu, g_out):
    f32 = jnp.float32
    b, T, _ = u.shape
    uf = u.astype(f32).reshape(b, T, S5_GROUPS, S5_GROUP)
    ar, ai = A_re.astype(f32), A_im.astype(f32)
    dt = jnp.exp(log_dt.astype(f32))[:, None]
    mag = jnp.exp(ar * dt)
    lam_r, lam_i = mag * jnp.cos(ai * dt), mag * jnp.sin(ai * dt)
    den = ar * ar + ai * ai
    zr, zi = lam_r - 1.0, lam_i
    fr = (zr * ar + zi * ai) / den
    fi = (zi * ar - zr * ai) / den
    br, bi = B_re.astype(f32), B_im.astype(f32)
    bbr = fr[..., None] * br - fi[..., None] * bi
    bbi = fr[..., None] * bi + fi[..., None] * br
    bu_r = jnp.einsum('btgc,gpc->btgp', uf, bbr)
    bu_i = jnp.einsum('btgc,gpc->btgp', uf, bbi)
    if h0 is not None:
        h0r, h0i = h0[0].astype(f32), h0[1].astype(f32)
        bu_r = bu_r.at[:, 0].add(lam_r * h0r - lam_i * h0i)
        bu_i = bu_i.at[:, 0].add(lam_r * h0i + lam_i * h0r)
    a_r = jnp.broadcast_to(lam_r, bu_r.shape)
    a_i = jnp.broadcast_to(lam_i, bu_i.shape)
    _, _, xr, xi = lax.associative_scan(_complex_affine_combine, (a_r, a_i, bu_r, bu_i), axis=1)
    y = (jnp.einsum('btgp,gcp->btgc', xr, C_re.astype(f32))
         - jnp.einsum('btgp,gcp->btgc', xi, C_im.astype(f32)))
    y = y.reshape(b, T, S5_WIDTH) + d_skip.astype(f32) * uf.reshape(b, T, S5_WIDTH)
    y = jax.nn.gelu(y)
    ya, yg = jnp.split(y @ w_glu.astype(f32), 2, axis=-1)
    out = _rmsnorm(ya * jax.nn.sigmoid(yg), g_out).astype(u.dtype)
    return out, xr[:, -1], xi[:, -1]


def _causal_conv(x, buf, w, bias):
    b, T, ch = x.shape
    if buf is None:
        buf = jnp.zeros((b, ML_CONV - 1, ch), x.dtype)
    xp = jnp.concatenate([buf.astype(x.dtype), x], axis=1)
    y = bias
    for j in range(ML_CONV):
        y = y + xp[:, j:j + T] * w[j]
    return y, xp[:, T:]


def _mlstm_cell(q, k, v, ig, lf, C0, n0, m0):
    b, T, H, _ = q.shape
    L = ML_CHUNK if T % ML_CHUNK == 0 else T
    nc = T // L

    def chunks(a):
        return a.reshape((b, nc, L) + a.shape[2:]).swapaxes(0, 1)

    tril = jnp.tril(jnp.ones((L, L), dtype=bool))

    def step(carry, xs):
        C, n, m = carry
        qc, kc, vc, ic, fc = xs
        bc = jnp.cumsum(fc, axis=1)
        d = bc[:, :, None, :] - bc[:, None, :, :] + ic[:, None, :, :]
        d = jnp.where(tril[None, :, :, None], d, -jnp.inf)
        inter = bc + m[:, None, :]
        m_tok = jnp.maximum(inter, jnp.max(d, axis=2))
        w_intra = jnp.exp(d - m_tok[:, :, None, :])
        w_inter = jnp.exp(inter - m_tok)
        s = jnp.einsum('bjhd,bshd->bjsh', qc, kc) * w_intra
        num = (jnp.einsum('bjsh,bshv->bjhv', s, vc)
               + w_inter[..., None] * jnp.einsum('bjhd,bhdv->bjhv', qc, C))
        den = jnp.sum(s, axis=2) + w_inter * jnp.einsum('bjhd,bhd->bjh', qc, n)
        h = num / jnp.maximum(jnp.abs(den), jnp.exp(-m_tok))[..., None]
        m_end = m_tok[:, -1]
        g_end = bc[:, -1]
        w_s = jnp.exp(g_end[:, None, :] - bc + ic - m_end[:, None, :])
        decay = jnp.exp(g_end + m - m_end)
        C_new = decay[..., None, None] * C + jnp.einsum('bsh,bshd,bshv->bhdv', w_s, kc, vc)
        n_new = decay[..., None] * n + jnp.einsum('bsh,bshd->bhd', w_s, kc)
        return (C_new, n_new, m_end), h

    (C, n, m), h = lax.scan(step, (C0, n0, m0), (chunks(q), chunks(k), chunks(v), chunks(ig), chunks(lf)))
    return h.swapaxes(0, 1).reshape(b, T, H, ML_DV), C, n, m


def _mlstm_mixer(zq, zk, zv, zi, zf, zo, conv_buf, state, conv_w, conv_b, b_i, b_f, g_norm):
    f32 = jnp.float32
    b, T, _ = zq.shape
    qk, new_buf = _causal_conv(jnp.concatenate([zq, zk], axis=-1), conv_buf, conv_w, conv_b)
    qk = jax.nn.silu(qk.astype(f32))
    q, k = jnp.split(qk, 2, axis=-1)
    q = q.reshape(b, T, ML_HEADS, ML_DK)
    k = k.reshape(b, T, ML_HEADS, ML_DK) * (ML_DK ** -0.5)
    v = zv.astype(f32).reshape(b, T, ML_HEADS, ML_DV)
    ig = zi.astype(f32) + b_i.astype(f32)
    lf = jax.nn.log_sigmoid(zf.astype(f32) + b_f.astype(f32))
    if state is None:
        C0 = jnp.zeros((b, ML_HEADS, ML_DK, ML_DV), f32)
        n0 = jnp.zeros((b, ML_HEADS, ML_DK), f32)
        m0 = jnp.zeros((b, ML_HEADS), f32)
    else:
        C0, n0, m0 = state[0].astype(f32), state[1].astype(f32), state[2].astype(f32)
    h, C, n, m = _mlstm_cell(q, k, v, ig, lf, C0, n0, m0)
    h = jax.nn.sigmoid(zo.astype(f32)).reshape(b, T, ML_HEADS, ML_DV) * h
    h = _rmsnorm(h, g_norm).reshape(b, T, ML_WIDTH).astype(zq.dtype)
    return h, new_buf, C, n, m


def _mla_project(zcq, zckv, zkr, pos, q_norm_g, w_uq, kv_norm_g, w_uk):
    b, T, _ = zcq.shape
    q = (_rmsnorm(zcq, q_norm_g) @ w_uq).reshape(b, T, MLA_HEADS, MLA_NOPE + MLA_ROPE)
    q_nope, q_rope = q[..., :MLA_NOPE], q[..., MLA_NOPE:]
    q_lat = jnp.einsum('bthn,chn->bthc', q_nope, w_uk)
    c = _rmsnorm(zckv, kv_norm_g)
    kr = _rope(zkr, pos)
    return q_lat, _rope(q_rope, pos), c, kr


def _mla_attend_prompt(q_lat, q_rope, c, kr):
    b, T, H, C = q_lat.shape
    nb = T // Q_BLOCK
    ql = q_lat.reshape(b, nb, Q_BLOCK, H, C).swapaxes(0, 1)
    qr = q_rope.reshape(b, nb, Q_BLOCK, H, MLA_ROPE).swapaxes(0, 1)
    kpos = jnp.arange(T, dtype=jnp.int32)

    def block(args):
        i, qlb, qrb = args
        s = (jnp.einsum('bqhc,bkc->bhqk', qlb, c) + jnp.einsum('bqhr,bkr->bhqk', qrb, kr)).astype(jnp.float32) * MLA_SCALE
        qpos = i * Q_BLOCK + jnp.arange(Q_BLOCK, dtype=jnp.int32)
        s = jnp.where((kpos[None, :] <= qpos[:, None])[None, None], s, -jnp.inf)
        p = jax.nn.softmax(s, axis=-1).astype(c.dtype)
        return jnp.einsum('bhqk,bkc->bqhc', p, c)

    o = lax.map(block, (jnp.arange(nb, dtype=jnp.int32), ql, qr))
    return o.swapaxes(0, 1).reshape(b, T, H, C)


def _mla_attend_sample(q_lat, q_rope, c_new, kr_new, c_past, kr_past):
    T = q_lat.shape[1]
    n_past = c_past.shape[1]
    s_past = jnp.einsum('bqhc,bkc->bhqk', q_lat, c_past) + jnp.einsum('bqhr,bkr->bhqk', q_rope, kr_past)
    s_new = jnp.einsum('bqhc,bkc->bhqk', q_lat, c_new) + jnp.einsum('bqhr,bkr->bhqk', q_rope, kr_new)
    tril = jnp.tril(jnp.ones((T, T), dtype=bool))
    s_new = jnp.where(tril[None, None], s_new.astype(jnp.float32), -jnp.inf)
    s = jnp.concatenate([s_past.astype(jnp.float32), s_new], axis=-1) * MLA_SCALE
    p = jax.nn.softmax(s, axis=-1).astype(c_new.dtype)
    return (jnp.einsum('bhqk,bkc->bqhc', p[..., :n_past], c_past)
            + jnp.einsum('bhqk,bkc->bqhc', p[..., n_past:], c_new))


def _gather_pages(pool, page_table):
    g = pool[page_table]
    return g.reshape(g.shape[0], g.shape[1] * g.shape[2], g.shape[3])


def _mem_kv(mem, g, w_k, w_v):
    b, m_len, _ = mem.shape
    m = _rmsnorm(mem, g)
    return ((m @ w_k).reshape(b, m_len, CA_HEADS, CA_HD), (m @ w_v).reshape(b, m_len, CA_HEADS, CA_HD))


def _cross_attn(h, mem_k, mem_v, w_q, w_o):
    b, T, _ = h.shape
    q = (h @ w_q).reshape(b, T, CA_HEADS, CA_HD)
    s = jnp.einsum('bthd,bmhd->bhtm', q, mem_k).astype(jnp.float32) * (CA_HD ** -0.5)
    p = jax.nn.softmax(s, axis=-1).astype(mem_v.dtype)
    o = jnp.einsum('bhtm,bmhd->bthd', p, mem_v).reshape(b, T, CA_WIDTH)
    return o @ w_o


def _sqrelu_mlp(h, w1, w2):
    a = jax.nn.relu(h @ w1)
    return (a * a) @ w2


def _layer(x, p, pos, mem_k, mem_v, ssm_h0, ml_state, conv_buf, kv_past):
    b, T, _ = x.shape
    h = _rmsnorm(x, p['norm_mix_g'])
    u, zq, zk, zv, zi, zf, zo, zcq, zckv, zkr = _split(h @ p['w_in'], IN_SIZES)
    y_s5, s_re, s_im = _s5_mixer(u, ssm_h0, p['s5_A_re'], p['s5_A_im'], p['s5_log_dt'], p['s5_B_re'], p['s5_B_im'],
                                 p['s5_C_re'], p['s5_C_im'], p['s5_D'], p['s5_w_glu'], p['s5_out_g'])
    y_ml, conv_new, C, n, m = _mlstm_mixer(zq, zk, zv, zi, zf, zo, conv_buf, ml_state, p['ml_conv_w'], p['ml_conv_b'],
                                           p['ml_b_i'], p['ml_b_f'], p['ml_norm_g'])
    q_lat, q_rope, c, kr = _mla_project(zcq, zckv, zkr, pos, p['mla_q_norm_g'], p['mla_w_uq'], p['mla_kv_norm_g'], p['mla_w_uk'])
    if kv_past is None:
        o_lat = _mla_attend_prompt(q_lat, q_rope, c, kr)
    else:
        o_lat = _mla_attend_sample(q_lat, q_rope, c, kr, kv_past[0], kv_past[1])
    y_mla = jnp.einsum('bthc,chv->bthv', o_lat, p['mla_w_uv']).reshape(b, T, MLA_WIDTH)
    y_mla = _rmsnorm(y_mla, p['mla_out_g'])
    x = x + jnp.concatenate([y_s5, y_ml, y_mla], axis=-1) @ p['w_out']
    x = x + _cross_attn(_rmsnorm(x, p['norm_ca_g']), mem_k, mem_v, p['ca_w_q'], p['ca_w_o'])
    x = x + _sqrelu_mlp(_rmsnorm(x, p['norm_ffn_g']), p['ffn_w1'], p['ffn_w2'])
    return x, (s_re, s_im, C, n, m, conv_new, c, kr)


def setup_inputs(seed: int = 0) -> dict:
    key = jax.random.key(seed)
    keys = iter(jax.random.split(key, 80))
    f32 = jnp.float32

    def nrm(shape, scale):
        return scale * jax.random.normal(next(keys), shape, f32)

    def gain(shape):
        return 1.0 + 0.05 * jax.random.normal(next(keys), shape, f32)

    n_pages = PAST_LEN // PAGE_SIZE
    n_phys = (5 * DEC_BATCH * n_pages) // 4
    perm = jax.random.permutation(next(keys), n_phys)
    page_table = perm[:DEC_BATCH * n_pages].reshape(DEC_BATCH, n_pages).astype(jnp.int32)

    a_im_base = jnp.pi * jnp.arange(S5_P, dtype=f32)
    return {
        'x_prompt': nrm((BATCH, SEQ, D_MODEL), 1.0),
        'x_sample': nrm((DEC_BATCH, DEC_SEQ, D_MODEL), 1.0),
        'state_ssm_re': nrm((DEPTH, DEC_BATCH, S5_GROUPS, S5_P), 0.3),
        'state_ssm_im': nrm((DEPTH, DEC_BATCH, S5_GROUPS, S5_P), 0.3),
        'state_mlstm_C': nrm((DEPTH, DEC_BATCH, ML_HEADS, ML_DK, ML_DV), 0.3),
        'state_mlstm_n': nrm((DEPTH, DEC_BATCH, ML_HEADS, ML_DK), 0.3),
        'state_mlstm_m': nrm((DEPTH, DEC_BATCH, ML_HEADS), 0.5),
        'state_mlstm_conv': nrm((DEPTH, DEC_BATCH, ML_CONV - 1, 2 * ML_QK), 1.0),
        'cache_kv_latent': nrm((DEPTH, n_phys, PAGE_SIZE, MLA_KV_LORA), 1.0),
        'cache_k_rope': nrm((DEPTH, n_phys, PAGE_SIZE, MLA_ROPE), 1.0),
        'cache_mem_k': nrm((DEPTH, DEC_BATCH, N_MEM, CA_HEADS, CA_HD), 1.0),
        'cache_mem_v': nrm((DEPTH, DEC_BATCH, N_MEM, CA_HEADS, CA_HD), 1.0),
        'page_table': page_table,
        'mem_prompt': nrm((BATCH, N_MEM, D_MODEL), 1.0),
        'norm_mix_g': gain((DEPTH, D_MODEL)),
        'w_in': nrm((DEPTH, D_MODEL, N_IN), D_MODEL ** -0.5),
        's5_A_re': -0.5 + nrm((DEPTH, S5_GROUPS, S5_P), 0.01),
        's5_A_im': a_im_base + nrm((DEPTH, S5_GROUPS, S5_P), 0.01),
        's5_log_dt': jax.random.uniform(next(keys), (DEPTH, S5_GROUPS), f32, math.log(1e-3), math.log(1e-1)),
        's5_B_re': nrm((DEPTH, S5_GROUPS, S5_P, S5_GROUP), (2 * S5_GROUP) ** -0.5),
        's5_B_im': nrm((DEPTH, S5_GROUPS, S5_P, S5_GROUP), (2 * S5_GROUP) ** -0.5),
        's5_C_re': nrm((DEPTH, S5_GROUPS, S5_GROUP, S5_P), (2 * S5_P) ** -0.5),
        's5_C_im': nrm((DEPTH, S5_GROUPS, S5_GROUP, S5_P), (2 * S5_P) ** -0.5),
        's5_D': nrm((DEPTH, S5_WIDTH), 1.0),
        's5_w_glu': nrm((DEPTH, S5_WIDTH, 2 * S5_WIDTH), S5_WIDTH ** -0.5),
        's5_out_g': gain((DEPTH, S5_WIDTH)),
        'ml_conv_w': nrm((DEPTH, ML_CONV, 2 * ML_QK), ML_CONV ** -0.5),
        'ml_conv_b': nrm((DEPTH, 2 * ML_QK), 0.01),
        'ml_b_i': nrm((DEPTH, ML_HEADS), 0.1),
        'ml_b_f': jnp.linspace(3.0, 6.0, ML_HEADS, dtype=f32)[None, :] + nrm((DEPTH, ML_HEADS), 0.1),
        'ml_norm_g': gain((DEPTH, ML_HEADS, ML_DV)),
        'mla_q_norm_g': gain((DEPTH, MLA_Q_LORA)),
        'mla_w_uq': nrm((DEPTH, MLA_Q_LORA, MLA_HEADS * (MLA_NOPE + MLA_ROPE)), MLA_Q_LORA ** -0.5),
        'mla_kv_norm_g': gain((DEPTH, MLA_KV_LORA)),
        'mla_w_uk': nrm((DEPTH, MLA_KV_LORA, MLA_HEADS, MLA_NOPE), MLA_KV_LORA ** -0.5),
        'mla_w_uv': nrm((DEPTH, MLA_KV_LORA, MLA_HEADS, MLA_V), MLA_KV_LORA ** -0.5),
        'mla_out_g': gain((DEPTH, MLA_WIDTH)),
        'w_out': nrm((DEPTH, MIX_WIDTH, D_MODEL), MIX_WIDTH ** -0.5),
        'norm_ca_g': gain((DEPTH, D_MODEL)),
        'ca_mem_g': gain((DEPTH, D_MODEL)),
        'ca_w_q': nrm((DEPTH, D_MODEL, CA_WIDTH), D_MODEL ** -0.5),
        'ca_w_k': nrm((DEPTH, D_MODEL, CA_WIDTH), D_MODEL ** -0.5),
        'ca_w_v': nrm((DEPTH, D_MODEL, CA_WIDTH), D_MODEL ** -0.5),
        'ca_w_o': nrm((DEPTH, CA_WIDTH, D_MODEL), CA_WIDTH ** -0.5),
        'norm_ffn_g': gain((DEPTH, D_MODEL)),
        'ffn_w1': nrm((DEPTH, D_MODEL, D_FF), D_MODEL ** -0.5),
        'ffn_w2': nrm((DEPTH, D_FF, D_MODEL), D_FF ** -0.5),
        'final_norm_g': gain((D_MODEL,)),
    }


def reference(x_prompt, x_sample, state_ssm_re, state_ssm_im, state_mlstm_C, state_mlstm_n, state_mlstm_m,
              state_mlstm_conv, cache_kv_latent, cache_k_rope, cache_mem_k, cache_mem_v, page_table, mem_prompt,
              norm_mix_g, w_in, s5_A_re, s5_A_im, s5_log_dt, s5_B_re, s5_B_im, s5_C_re, s5_C_im, s5_D, s5_w_glu,
              s5_out_g, ml_conv_w, ml_conv_b, ml_b_i, ml_b_f, ml_norm_g, mla_q_norm_g, mla_w_uq, mla_kv_norm_g,
              mla_w_uk, mla_w_uv, mla_out_g, w_out, norm_ca_g, ca_mem_g, ca_w_q, ca_w_k, ca_w_v, ca_w_o,
              norm_ffn_g, ffn_w1, ffn_w2, final_norm_g):
    params = [dict(norm_mix_g=norm_mix_g[l], w_in=w_in[l], s5_A_re=s5_A_re[l], s5_A_im=s5_A_im[l],
                   s5_log_dt=s5_log_dt[l], s5_B_re=s5_B_re[l], s5_B_im=s5_B_im[l], s5_C_re=s5_C_re[l],
                   s5_C_im=s5_C_im[l], s5_D=s5_D[l], s5_w_glu=s5_w_glu[l], s5_out_g=s5_out_g[l],
                   ml_conv_w=ml_conv_w[l], ml_conv_b=ml_conv_b[l], ml_b_i=ml_b_i[l], ml_b_f=ml_b_f[l],
                   ml_norm_g=ml_norm_g[l], mla_q_norm_g=mla_q_norm_g[l], mla_w_uq=mla_w_uq[l],
                   mla_kv_norm_g=mla_kv_norm_g[l], mla_w_uk=mla_w_uk[l], mla_w_uv=mla_w_uv[l],
                   mla_out_g=mla_out_g[l], w_out=w_out[l], norm_ca_g=norm_ca_g[l], ca_mem_g=ca_mem_g[l],
                   ca_w_q=ca_w_q[l], ca_w_k=ca_w_k[l], ca_w_v=ca_w_v[l], ca_w_o=ca_w_o[l],
                   norm_ffn_g=norm_ffn_g[l], ffn_w1=ffn_w1[l], ffn_w2=ffn_w2[l]) for l in range(DEPTH)]

    pos_p = jnp.arange(x_prompt.shape[1], dtype=jnp.int32)
    xp = x_prompt
    p_states = []
    for l in range(DEPTH):
        p = params[l]
        mk, mv = _mem_kv(mem_prompt, p['ca_mem_g'], p['ca_w_k'], p['ca_w_v'])
        xp, st = _layer(xp, p, pos_p, mk, mv, None, None, None, None)
        p_states.append(st + (mk, mv))
    y_prompt = _rmsnorm(xp, final_norm_g)
    (p_ssm_re, p_ssm_im, p_mlstm_C, p_mlstm_n, p_mlstm_m, p_mlstm_conv, p_kv_latent, p_k_rope,
     p_mem_k, p_mem_v) = [jnp.stack([s[i] for s in p_states]) for i in range(10)]

    pos_s = PAST_LEN + jnp.arange(x_sample.shape[1], dtype=jnp.int32)
    xs = x_sample
    s_states = []
    for l in range(DEPTH):
        p = params[l]
        kv_past = (_gather_pages(cache_kv_latent[l], page_table), _gather_pages(cache_k_rope[l], page_table))
        xs, st = _layer(xs, p, pos_s, cache_mem_k[l], cache_mem_v[l], (state_ssm_re[l], state_ssm_im[l]),
                        (state_mlstm_C[l], state_mlstm_n[l], state_mlstm_m[l]), state_mlstm_conv[l], kv_past)
        s_states.append(st)
    y_sample = _rmsnorm(xs, final_norm_g)
    (s_ssm_re, s_ssm_im, s_mlstm_C, s_mlstm_n, s_mlstm_m, s_mlstm_conv, s_kv_latent,
     s_k_rope) = [jnp.stack([s[i] for s in s_states]) for i in range(8)]

    return (y_prompt, y_sample, p_ssm_re, p_ssm_im, p_mlstm_C, p_mlstm_n, p_mlstm_m, p_mlstm_conv, p_kv_latent,
            p_k_rope, p_mem_k, p_mem_v, s_ssm_re, s_ssm_im, s_mlstm_C, s_mlstm_n, s_mlstm_m, s_mlstm_conv,
            s_kv_latent, s_k_rope)
```

```python
import functools
import math

import jax
import jax.numpy as jnp
from jax import lax
from jax.experimental import pallas as pl
from jax.experimental.pallas import tpu as pltpu

F32 = jnp.float32
BF16 = jnp.bfloat16
NORM_EPS = 1e-6
ROPE_BASE = 10000.0

S5_CH = 16
S5_STATE = 64
ML_HEADS = 4
ML_HD = 64
ML_CONV = 4
MLA_HEADS = 8
MLA_NOPE = 64
MLA_ROPE = 32
MLA_V = 64
CA_HEADS = 4
CA_HD = 64
QK_PAD = 256

COL_U, COL_Q, COL_K, COL_V, COL_O, COL_CQ, COL_CKV, COL_MISC = 0, 256, 512, 768, 1024, 1280, 1536, 1664
N_IN_PAD = 1792
MISC_KR, MISC_IG, MISC_FG = 0, 32, 36

VMEM_LIMIT = 56 * 1024 * 1024


def _cparams(*sem):
    return pltpu.CompilerParams(dimension_semantics=sem, vmem_limit_bytes=VMEM_LIMIT)


def _rms(x, g):
    return x * lax.rsqrt(jnp.mean(x * x, axis=-1, keepdims=True) + NORM_EPS) * g


def _dot(a, b):
    return jnp.dot(a, b, preferred_element_type=F32)


def _dot_nt(a, b):
    return lax.dot_general(a, b, (((1,), (1,)), ((), ())), preferred_element_type=F32)


def _dot_tn(a, b):
    return lax.dot_general(a, b, (((0,), (0,)), ((), ())), preferred_element_type=F32)


def _const_spec(shape):
    nd = len(shape)
    return pl.BlockSpec(shape, lambda *_: (0,) * nd)


def _linear_body(*refs, n_in, has_norm, has_res):
    x_refs = refs[:n_in]
    pos = n_in
    g_ref = refs[pos] if has_norm else None
    pos += int(has_norm)
    w_refs = refs[pos:pos + n_in]
    pos += n_in
    res_ref = refs[pos] if has_res else None
    pos += int(has_res)
    o_ref = refs[pos]
    acc = None
    for x_ref, w_ref in zip(x_refs, w_refs):
        x = x_ref[...]
        if has_norm:
            x = _rms(x, g_ref[...])
        p = _dot(x.astype(BF16), w_ref[...])
        acc = p if acc is None else acc + p
    if has_res:
        acc = acc + res_ref[...]
    o_ref[...] = acc


def _linear(xs, ws, gamma=None, residual=None, tm=512):
    m = xs[0].shape[0]
    n = ws[0].shape[1]
    tm = min(tm, m)
    assert m % tm == 0
    in_specs = [pl.BlockSpec((tm, x.shape[1]), lambda i: (i, 0)) for x in xs]
    args = list(xs)
    if gamma is not None:
        in_specs.append(_const_spec((1, xs[0].shape[1])))
        args.append(gamma.reshape(1, -1))
    for w in ws:
        in_specs.append(_const_spec(w.shape))
        args.append(w)
    if residual is not None:
        in_specs.append(pl.BlockSpec((tm, n), lambda i: (i, 0)))
        args.append(residual)
    body = functools.partial(_linear_body, n_in=len(xs), has_norm=gamma is not None, has_res=residual is not None)
    return pl.pallas_call(
        body,
        grid=(m // tm,),
        in_specs=in_specs,
        out_specs=pl.BlockSpec((tm, n), lambda i: (i, 0)),
        out_shape=jax.ShapeDtypeStruct((m, n), F32),
        compiler_params=_cparams("parallel"),
        name="linear",
    )(*args)


def _mlp_body(x_ref, g_ref, w1_ref, w2_ref, gf_ref, o_ref, *, fc, final_norm):
    x = x_ref[...]
    h = _rms(x, g_ref[...]).astype(BF16)
    acc = x
    for c in range(w1_ref.shape[1] // fc):
        a = _dot(h, w1_ref[:, c * fc:(c + 1) * fc])
        a = jnp.maximum(a, 0.0)
        acc = acc + _dot((a * a).astype(BF16), w2_ref[c * fc:(c + 1) * fc, :])
    if final_norm:
        acc = _rms(acc, gf_ref[...])
    o_ref[...] = acc


def _mlp(x, gamma, w1, w2, final_gamma, final_norm, tm=512, fc=1024):
    m, d = x.shape
    dff = w1.shape[1]
    tm = min(tm, m)
    fc = min(fc, dff)
    body = functools.partial(_mlp_body, fc=fc, final_norm=final_norm)
    return pl.pallas_call(
        body,
        grid=(m // tm,),
        in_specs=[
            pl.BlockSpec((tm, d), lambda i: (i, 0)),
            _const_spec((1, d)),
            pl.BlockSpec((d, dff), lambda i: (0, 0), pipeline_mode=pl.Buffered(1)),
            pl.BlockSpec((dff, d), lambda i: (0, 0), pipeline_mode=pl.Buffered(1)),
            _const_spec((1, d)),
        ],
        out_specs=pl.BlockSpec((tm, d), lambda i: (i, 0)),
        out_shape=jax.ShapeDtypeStruct((m, d), F32),
        compiler_params=_cparams("parallel"),
        name="mlp",
    )(x, gamma.reshape(1, -1), w1, w2, final_gamma.reshape(1, -1))


def _s5_body(u_ref, h0r_ref, h0i_ref, are_ref, aim_ref, ldt_ref, br_ref, bi_ref, cr_ref, ci_ref, d_ref, wg_ref,
             go_ref, y_ref, hro_ref, hio_ref, xr_s, xi_s, hr_s, hi_s, *, tc, bb):
    @pl.when(pl.program_id(1) == 0)
    def _():
        hr_s[...] = h0r_ref[...]
        hi_s[...] = h0i_ref[...]

    ar = are_ref[...]
    ai = aim_ref[...]
    dt = jnp.exp(ldt_ref[...])
    mag = jnp.exp(ar * dt)
    lr = mag * jnp.cos(ai * dt)
    li = mag * jnp.sin(ai * dt)
    den = ar * ar + ai * ai
    zr = lr - 1.0
    fr = (zr * ar + li * ai) / den
    fi = (li * ar - zr * ai) / den

    ch = u_ref.shape[2]
    u = u_ref[...].reshape(tc * bb, ch)
    ub = u.astype(BF16)
    pr = _dot(ub, br_ref[...])
    pi = _dot(ub, bi_ref[...])
    xr_s[...] = fr * pr - fi * pi
    xi_s[...] = fr * pi + fi * pr

    def step(t, carry):
        hr, hi = carry
        rows = pl.ds(pl.multiple_of(t * bb, bb), bb)
        nr = lr * hr - li * hi + xr_s[rows, :]
        ni = lr * hi + li * hr + xi_s[rows, :]
        xr_s[rows, :] = nr
        xi_s[rows, :] = ni
        return nr, ni

    hr, hi = lax.fori_loop(0, tc, step, (hr_s[...], hi_s[...]))
    hr_s[...] = hr
    hi_s[...] = hi
    hro_ref[...] = hr
    hio_ref[...] = hi

    y = _dot(xr_s[...].astype(BF16), cr_ref[...]) - _dot(xi_s[...].astype(BF16), ci_ref[...])
    y = jax.nn.gelu(y + d_ref[...] * u)
    g = _dot(y.astype(BF16), wg_ref[...])
    o = g[:, :ch] * jax.nn.sigmoid(g[:, ch:])
    y_ref[...] = _rms(o, go_ref[...]).reshape(tc, bb, ch)


def _s5(u_t, h0r, h0i, w, tc):
    t, b, ch = u_t.shape
    gp = h0r.shape[1]
    bb = b if b <= 128 else 128
    tc = min(tc, t)
    body = functools.partial(_s5_body, tc=tc, bb=bb)
    row = lambda n: _const_spec((1, n))
    return pl.pallas_call(
        body,
        grid=(b // bb, t // tc),
        in_specs=[
            pl.BlockSpec((tc, bb, ch), lambda i, j: (j, i, 0)),
            pl.BlockSpec((bb, gp), lambda i, j: (i, 0)),
            pl.BlockSpec((bb, gp), lambda i, j: (i, 0)),
            row(gp), row(gp), row(gp),
            _const_spec((ch, gp)), _const_spec((ch, gp)),
            _const_spec((gp, ch)), _const_spec((gp, ch)),
            row(ch), _const_spec((ch, 2 * ch)), row(ch),
        ],
        out_specs=[
            pl.BlockSpec((tc, bb, ch), lambda i, j: (j, i, 0)),
            pl.BlockSpec((bb, gp), lambda i, j: (i, 0)),
            pl.BlockSpec((bb, gp), lambda i, j: (i, 0)),
        ],
        out_shape=[
            jax.ShapeDtypeStruct((t, b, ch), F32),
            jax.ShapeDtypeStruct((b, gp), F32),
            jax.ShapeDtypeStruct((b, gp), F32),
        ],
        scratch_shapes=[
            pltpu.VMEM((tc * bb, gp), F32), pltpu.VMEM((tc * bb, gp), F32),
            pltpu.VMEM((bb, gp), F32), pltpu.VMEM((bb, gp), F32),
        ],
        compiler_params=_cparams("parallel", "arbitrary"),
        name="s5",
    )(u_t, h0r, h0i, w["s5_are"], w["s5_aim"], w["s5_ldt"], w["s5_br"], w["s5_bi"], w["s5_cr"], w["s5_ci"],
      w["s5_d"], w["s5_wglu"], w["s5_go"])


def _split3(x):
    hi = x.astype(BF16)
    r1 = x - hi.astype(F32)
    mid = r1.astype(BF16)
    lo = (r1 - mid.astype(F32)).astype(BF16)
    return hi, mid, lo


def _mlstm_body(zq_ref, zk_ref, zv_ref, zo_ref, misc_ref, gt_ref, conv0_ref, c0_ref, m0_ref, cw_ref, cb_ref,
                bcol_ref, brow_ref, gn_ref, y_ref, co_ref, mo_ref, padq, padk, cs, ms, *, cl):
    hd = ML_HD
    qw = ML_HEADS * hd

    @pl.when(pl.program_id(1) == 0)
    def _():
        padq[0:8, :] = conv0_ref[0][:, :qw]
        padk[0:8, :] = conv0_ref[0][:, qw:]
        cs[...] = c0_ref[0]
        ms[...] = m0_ref[0]

    padq[8:8 + cl, :] = zq_ref[0]
    padk[8:8 + cl, :] = zk_ref[0]
    cw = cw_ref[...]
    cb = cb_ref[...]

    def conv(pad, w, b):
        y = b
        for j in range(ML_CONV):
            y = y + pad[8 - (ML_CONV - 1) + j:8 - (ML_CONV - 1) + j + cl, :] * w[j:j + 1, :]
        return y

    q = jax.nn.silu(conv(padq, cw[:, :qw], cb[:, :qw]))
    k = jax.nn.silu(conv(padk, cw[:, qw:], cb[:, qw:])) * (hd ** -0.5)
    if cl >= 8:
        padq[0:8, :] = padq[cl:cl + 8, :]
        padk[0:8, :] = padk[cl:cl + 8, :]

    gcol = misc_ref[0] + bcol_ref[...]
    lf_col = jax.nn.log_sigmoid(gcol)
    grow = gt_ref[0] + brow_ref[:, 0:1]
    ig_row = grow[0:ML_HEADS]
    lf_row = jax.nn.log_sigmoid(grow[ML_HEADS:2 * ML_HEADS])

    ri = lax.broadcasted_iota(jnp.int32, (cl, cl), 0)
    ci = lax.broadcasted_iota(jnp.int32, (cl, cl), 1)
    tril = ri >= ci
    lower = jnp.where(tril, 1.0, 0.0).astype(BF16)
    upper = jnp.where(ri <= ci, 1.0, 0.0).astype(BF16)
    bc_col = sum(_dot(lower, part) for part in _split3(lf_col))
    bc_row = sum(_dot(part, upper) for part in _split3(lf_row))

    v = zv_ref[0]
    og = jax.nn.sigmoid(zo_ref[0])
    gn = gn_ref[...]
    one_col = jnp.where(lax.broadcasted_iota(jnp.int32, (cl, hd), 1) == 0, 1.0, 0.0)
    outs = []
    for h in range(ML_HEADS):
        sl = slice(h * hd, (h + 1) * hd)
        qh = q[:, sl].astype(BF16)
        kh = k[:, sl]
        v_aug = jnp.concatenate([v[:, sl], one_col], axis=1).astype(BF16)
        bcj = bc_col[:, MISC_FG + h:MISC_FG + h + 1]
        igj = gcol[:, MISC_IG + h:MISC_IG + h + 1]
        m_prev = ms[h:h + 1, 0:1]
        d = bcj - bc_row[h:h + 1, :] + ig_row[h:h + 1, :]
        d = jnp.where(tril, d, -jnp.inf)
        inter = bcj + m_prev
        m_tok = jnp.maximum(inter, jnp.max(d, axis=1, keepdims=True))
        w_intra = jnp.exp(d - m_tok)
        w_inter = jnp.exp(inter - m_tok)
        s = _dot_nt(qh, kh.astype(BF16)) * w_intra
        c_aug = cs[h]
        nd = _dot(s.astype(BF16), v_aug) + w_inter * _dot(qh, c_aug.astype(BF16))
        num = nd[:, :hd]
        den = nd[:, hd:hd + 1]
        hh = num / jnp.maximum(jnp.abs(den), jnp.exp(-m_tok))
        m_end = m_tok[cl - 1:cl, :]
        g_end = bcj[cl - 1:cl, :]
        w_s = jnp.exp(g_end - bcj + igj - m_end)
        decay = jnp.exp(g_end + m_prev - m_end)
        cs[h] = decay * c_aug + _dot_tn((kh * w_s).astype(BF16), v_aug)
        ms[h:h + 1, :] = jnp.broadcast_to(m_end, (1, ms.shape[1]))
        oh = og[:, sl] * hh
        outs.append(oh * lax.rsqrt(jnp.mean(oh * oh, axis=-1, keepdims=True) + NORM_EPS) * gn[:, sl])
    y_ref[0] = jnp.concatenate(outs, axis=1)
    co_ref[0] = cs[...]
    mo_ref[0] = ms[...]


def _mlstm(z3, gates_t, conv0, c0_aug, m0, w, cl):
    b, t, _ = z3.shape
    qw = ML_HEADS * ML_HD
    cl = min(cl, t)
    body = functools.partial(_mlstm_body, cl=cl)
    zspec = lambda col: pl.BlockSpec((1, cl, qw), lambda i, j: (i, j, col // qw))
    return pl.pallas_call(
        body,
        grid=(b, t // cl),
        in_specs=[
            zspec(COL_Q), zspec(COL_K), zspec(COL_V), zspec(COL_O),
            pl.BlockSpec((1, cl, 128), lambda i, j: (i, j, COL_MISC // 128)),
            pl.BlockSpec((1, 8, cl), lambda i, j: (i, 0, j)),
            pl.BlockSpec((1, 8, 2 * qw), lambda i, j: (i, 0, 0)),
            pl.BlockSpec((1, ML_HEADS, ML_HD, 2 * ML_HD), lambda i, j: (i, 0, 0, 0)),
            pl.BlockSpec((1, 8, 128), lambda i, j: (i, 0, 0)),
            _const_spec((ML_CONV, 2 * qw)), _const_spec((1, 2 * qw)),
            _const_spec((1, 128)), _const_spec((8, 128)), _const_spec((1, qw)),
        ],
        out_specs=[
            pl.BlockSpec((1, cl, qw), lambda i, j: (i, j, 0)),
            pl.BlockSpec((1, ML_HEADS, ML_HD, 2 * ML_HD), lambda i, j: (i, 0, 0, 0)),
            pl.BlockSpec((1, 8, 128), lambda i, j: (i, 0, 0)),
        ],
        out_shape=[
            jax.ShapeDtypeStruct((b, t, qw), F32),
            jax.ShapeDtypeStruct((b, ML_HEADS, ML_HD, 2 * ML_HD), F32),
            jax.ShapeDtypeStruct((b, 8, 128), F32),
        ],
        scratch_shapes=[
            pltpu.VMEM((cl + 8, qw), F32), pltpu.VMEM((cl + 8, qw), F32),
            pltpu.VMEM((ML_HEADS, ML_HD, 2 * ML_HD), F32), pltpu.VMEM((8, 128), F32),
        ],
        compiler_params=_cparams("parallel", "arbitrary"),
        name="mlstm",
    )(z3, z3, z3, z3, z3, gates_t, conv0, c0_aug, m0, w["ml_cw"], w["ml_cb"], w["ml_bcol"], w["ml_brow"], w["ml_gn"])


def _rope128(x, cos_t, sin_up, sin_dn):
    half = MLA_ROPE // 2
    return x * cos_t + pltpu.roll(x, half, 1) * sin_up + pltpu.roll(x, 128 - half, 1) * sin_dn


def _mla_prep_body(zcq_ref, zckv_ref, misc_ref, cos_ref, sup_ref, sdn_ref, gq_ref, wn_ref, wr_ref, wuk_ref, gkv_ref,
                   ql_ref, qr_ref, c_ref, kcat_ref, kr_ref):
    cos_t, sup, sdn = cos_ref[...], sup_ref[...], sdn_ref[...]
    cq = _rms(zcq_ref[...], gq_ref[...]).astype(BF16)
    qn = _dot(cq, wn_ref[...])
    qrp = _dot(cq, wr_ref[...])
    for j in range(wuk_ref.shape[0]):
        ql_ref[:, 256 * j:256 * (j + 1)] = _dot(qn[:, 128 * j:128 * (j + 1)].astype(BF16), wuk_ref[j]).astype(BF16)
    for j in range(qrp.shape[1] // 128):
        qr_ref[:, 128 * j:128 * (j + 1)] = _rope128(qrp[:, 128 * j:128 * (j + 1)], cos_t, sup, sdn).astype(BF16)
    c = _rms(zckv_ref[...], gkv_ref[...])
    c_ref[...] = c
    krf = _rope128(misc_ref[...], cos_t, sup, sdn)
    kr_ref[...] = krf[:, MISC_KR:MISC_KR + MLA_ROPE]
    lane = lax.broadcasted_iota(jnp.int32, krf.shape, 1)
    kcat_ref[:, 0:128] = c.astype(BF16)
    kcat_ref[:, 128:256] = jnp.where(lane < MLA_ROPE, krf, 0.0).astype(BF16)


def _mla_prep(z, tables, t_len, w, tm):
    m = z.shape[0]
    tm = min(tm, m, t_len) if t_len >= 8 else m
    nt = tables[0].shape[0] // tm
    tspec = pl.BlockSpec((tm, 128), lambda i: (i % nt, 0))
    kvl = 128
    return pl.pallas_call(
        _mla_prep_body,
        grid=(m // tm,),
        in_specs=[
            pl.BlockSpec((tm, 256), lambda i: (i, COL_CQ // 256)),
            pl.BlockSpec((tm, kvl), lambda i: (i, COL_CKV // 128)),
            pl.BlockSpec((tm, 128), lambda i: (i, COL_MISC // 128)),
            tspec, tspec, tspec,
            _const_spec((1, 256)), _const_spec(w["mla_wn"].shape), _const_spec(w["mla_wr"].shape),
            _const_spec(w["mla_wuk"].shape), _const_spec((1, kvl)),
        ],
        out_specs=[
            pl.BlockSpec((tm, MLA_HEADS * kvl), lambda i: (i, 0)),
            pl.BlockSpec((tm, MLA_HEADS * MLA_ROPE), lambda i: (i, 0)),
            pl.BlockSpec((tm, kvl), lambda i: (i, 0)),
            pl.BlockSpec((tm, QK_PAD), lambda i: (i, 0)),
            pl.BlockSpec((tm, MLA_ROPE), lambda i: (i, 0)),
        ],
        out_shape=[
            jax.ShapeDtypeStruct((m, MLA_HEADS * kvl), BF16),
            jax.ShapeDtypeStruct((m, MLA_HEADS * MLA_ROPE), BF16),
            jax.ShapeDtypeStruct((m, kvl), F32),
            jax.ShapeDtypeStruct((m, QK_PAD), BF16),
            jax.ShapeDtypeStruct((m, MLA_ROPE), F32),
        ],
        compiler_params=_cparams("parallel"),
        name="mla_prep",
    )(z, z, z, tables[0], tables[1], tables[2], w["mla_gq"], w["mla_wn"], w["mla_wr"], w["mla_wuk"], w["mla_gkv"])


def _mla_out(o, tq, wuv_ref, g_ref):
    ob = o.astype(BF16)
    y = None
    for h in range(MLA_HEADS):
        p = _dot(ob[h * tq:(h + 1) * tq], wuv_ref[h])
        y = p if y is None else y + p
    return _rms(y, g_ref[...])


def _mla_prompt_body(q_ref, k_ref, wuv_ref, g_ref, o_ref, m_s, l_s, acc_s, *, tq, scale):
    i = pl.program_id(1)
    rows = MLA_HEADS * tq
    q = q_ref[0].reshape(rows, q_ref.shape[3])
    m_s[...] = jnp.full(m_s.shape, -jnp.inf, F32)
    l_s[...] = jnp.zeros(l_s.shape, F32)
    acc_s[...] = jnp.zeros(acc_s.shape, F32)

    def block(j, masked):
        kb = k_ref[0, pl.ds(pl.multiple_of(j * tq, tq), tq), :]
        s = _dot_nt(q, kb) * scale
        if masked:
            r = lax.broadcasted_iota(jnp.int32, s.shape, 0) % tq
            c = lax.broadcasted_iota(jnp.int32, s.shape, 1)
            s = jnp.where(c <= r, s, -jnp.inf)
        m_old = m_s[...]
        m_new = jnp.maximum(m_old, jnp.max(s, axis=1, keepdims=True))
        alpha = jnp.exp(m_old - m_new)
        p = jnp.exp(s - m_new)
        l_s[...] = alpha * l_s[...] + jnp.sum(p, axis=1, keepdims=True)
        acc_s[...] = alpha * acc_s[...] + _dot(p.astype(BF16), kb[:, :acc_s.shape[1]])
        m_s[...] = m_new

    def full_block(j, carry):
        block(j, False)
        return carry

    lax.fori_loop(0, i, full_block, 0)
    block(i, True)
    o_ref[0] = _mla_out(acc_s[...] / l_s[...], tq, wuv_ref, g_ref)


def _mla_prompt(q, kcat, w, tq):
    b, hn, t, qk = q.shape
    tq = min(tq, t)
    kvl = 128
    wo = MLA_HEADS * MLA_V
    body = functools.partial(_mla_prompt_body, tq=tq, scale=1.0 / math.sqrt(MLA_NOPE + MLA_ROPE))
    return pl.pallas_call(
        body,
        grid=(b, t // tq),
        in_specs=[
            pl.BlockSpec((1, hn, tq, qk), lambda i, j: (i, 0, j, 0)),
            pl.BlockSpec((1, t, qk), lambda i, j: (i, 0, 0)),
            _const_spec(w["mla_wuv"].shape), _const_spec((1, wo)),
        ],
        out_specs=pl.BlockSpec((1, tq, wo), lambda i, j: (i, j, 0)),
        out_shape=jax.ShapeDtypeStruct((b, t, wo), F32),
        scratch_shapes=[pltpu.VMEM((hn * tq, 1), F32), pltpu.VMEM((hn * tq, 1), F32), pltpu.VMEM((hn * tq, kvl), F32)],
        compiler_params=_cparams("parallel", "parallel"),
        name="mla_prompt",
    )(q, kcat, w["mla_wuv"], w["mla_go"])


def _mla_sample_body(pt_ref, q_ref, kn_ref, *refs, pages, page, t_new, scale):
    c_refs = refs[:pages]
    r_refs = refs[pages:2 * pages]
    wuv_ref, g_ref, o_ref, kbuf, m_s, l_s, acc_s = refs[2 * pages:]
    j = pl.program_id(1)
    kvl = acc_s.shape[1]

    @pl.when(j == 0)
    def _():
        m_s[...] = jnp.full(m_s.shape, -jnp.inf, F32)
        l_s[...] = jnp.zeros(l_s.shape, F32)
        acc_s[...] = jnp.zeros(acc_s.shape, F32)

    zpad = jnp.zeros((page, QK_PAD - kvl - MLA_ROPE), F32)
    for i in range(pages):
        kbuf[i * page:(i + 1) * page, 0:kvl] = c_refs[i][0, 0].astype(BF16)
        kbuf[i * page:(i + 1) * page, kvl:QK_PAD] = jnp.concatenate([r_refs[i][0, 0], zpad], axis=1).astype(BF16)

    q = q_ref[0].astype(BF16)
    s = _dot_nt(q, kbuf[...]) * scale
    m_old = m_s[...]
    m_new = jnp.maximum(m_old, jnp.max(s, axis=1, keepdims=True))
    alpha = jnp.exp(m_old - m_new)
    p = jnp.exp(s - m_new)
    l_s[...] = alpha * l_s[...] + jnp.sum(p, axis=1, keepdims=True)
    acc_s[...] = alpha * acc_s[...] + _dot(p.astype(BF16), kbuf[:, 0:kvl])
    m_s[...] = m_new

    @pl.when(j == pl.num_programs(1) - 1)
    def _():
        kn = kn_ref[0].astype(F32)
        qf = q.astype(F32)
        t_row = lax.broadcasted_iota(jnp.int32, (q.shape[0], 1), 0) % t_new
        s_new = []
        for t2 in range(t_new):
            st = jnp.sum(qf * kn[t2:t2 + 1, :], axis=1, keepdims=True) * scale
            s_new.append(jnp.where(t_row >= t2, st, -jnp.inf))
        m_o = m_s[...]
        m_n = m_o
        for st in s_new:
            m_n = jnp.maximum(m_n, st)
        al = jnp.exp(m_o - m_n)
        l = al * l_s[...]
        acc = al * acc_s[...]
        for t2, st in enumerate(s_new):
            pt = jnp.exp(st - m_n)
            l = l + pt
            acc = acc + pt * kn[t2:t2 + 1, 0:kvl]
        acc_s[...] = acc / l
        y = None
        for h in range(MLA_HEADS):
            ph = _dot(acc_s[h * t_new:(h + 1) * t_new, :].astype(BF16), wuv_ref[h])
            y = ph if y is None else y + ph
        o_ref[0] = _rms(y, g_ref[...])


def _mla_sample(q, knew, cache_c, cache_r, layer, page_table, w, pages):
    b, rows, qk = q.shape
    t_new = knew.shape[1]
    n_pages = page_table.shape[1]
    page, kvl = cache_c.shape[2], cache_c.shape[3]
    pages = min(pages, n_pages)
    wo = MLA_HEADS * MLA_V

    def cmap(i):
        return lambda bi, j, pt: (layer, pt[bi, j * pages + i], 0, 0)

    in_specs = [
        pl.BlockSpec((1, rows, qk), lambda bi, j, pt: (bi, 0, 0)),
        pl.BlockSpec((1, t_new, qk), lambda bi, j, pt: (bi, 0, 0)),
    ]
    in_specs += [pl.BlockSpec((1, 1, page, kvl), cmap(i)) for i in range(pages)]
    in_specs += [pl.BlockSpec((1, 1, page, MLA_ROPE), cmap(i)) for i in range(pages)]
    in_specs += [pl.BlockSpec(w["mla_wuv"].shape, lambda bi, j, pt: (0, 0, 0)), pl.BlockSpec((1, wo), lambda bi, j, pt: (0, 0))]
    body = functools.partial(_mla_sample_body, pages=pages, page=page, t_new=t_new,
                             scale=1.0 / math.sqrt(MLA_NOPE + MLA_ROPE))
    grid_spec = pltpu.PrefetchScalarGridSpec(
        num_scalar_prefetch=1,
        grid=(b, n_pages // pages),
        in_specs=in_specs,
        out_specs=pl.BlockSpec((1, t_new, wo), lambda bi, j, pt: (bi, 0, 0)),
        scratch_shapes=[
            pltpu.VMEM((pages * page, QK_PAD), BF16),
            pltpu.VMEM((rows, 1), F32), pltpu.VMEM((rows, 1), F32), pltpu.VMEM((rows, kvl), F32),
        ],
    )
    return pl.pallas_call(
        body,
        grid_spec=grid_spec,
        out_shape=jax.ShapeDtypeStruct((b, t_new, wo), F32),
        compiler_params=_cparams("arbitrary", "arbitrary"),
        name="mla_sample",
    )(page_table, q, knew, *([cache_c] * pages), *([cache_r] * pages), w["mla_wuv"], w["mla_go"])


def _cross_body(q_ref, k_ref, v_ref, o_ref):
    q = q_ref[0]
    kb = k_ref[0].astype(BF16)
    vb = v_ref[0].astype(BF16)
    lane = lax.broadcasted_iota(jnp.int32, q.shape, 1)
    out = jnp.zeros(q.shape, F32)
    for h in range(CA_HEADS):
        sel = (lane >= h * CA_HD) & (lane < (h + 1) * CA_HD)
        s = _dot_nt(jnp.where(sel, q, 0.0).astype(BF16), kb) * (CA_HD ** -0.5)
        e = jnp.exp(s - jnp.max(s, axis=1, keepdims=True))
        p = e / jnp.sum(e, axis=1, keepdims=True)
        out = out + jnp.where(sel, _dot(p.astype(BF16), vb), 0.0)
    o_ref[0] = out


def _cross(q3, mem_k, mem_v, tt):
    b, t, wd = q3.shape
    nm = mem_k.shape[1]
    tt = min(tt, t)
    return pl.pallas_call(
        _cross_body,
        grid=(b, t // tt),
        in_specs=[
            pl.BlockSpec((1, tt, wd), lambda i, j: (i, j, 0)),
            pl.BlockSpec((1, nm, wd), lambda i, j: (i, 0, 0)),
            pl.BlockSpec((1, nm, wd), lambda i, j: (i, 0, 0)),
        ],
        out_specs=pl.BlockSpec((1, tt, wd), lambda i, j: (i, j, 0)),
        out_shape=jax.ShapeDtypeStruct((b, t, wd), F32),
        compiler_params=_cparams("parallel", "parallel"),
        name="cross_attn",
    )(q3, mem_k, mem_v)


def _prep_layer(p):
    w = {}
    wi = p["w_in"]
    d = wi.shape[0]
    o_i = 4 * 256
    w["w_in"] = jnp.concatenate(
        [wi[:, :o_i], wi[:, o_i + 8:o_i + 8 + 256 + 256 + 128 + 32], wi[:, o_i:o_i + 8],
         jnp.zeros((d, N_IN_PAD - wi.shape[1]), wi.dtype)], axis=1).astype(BF16)
    w["norm_mix_g"] = p["norm_mix_g"]

    g, pn = p["s5_A_re"].shape
    eye = jnp.eye(g, dtype=F32)
    w["s5_are"] = p["s5_A_re"].reshape(1, g * pn)
    w["s5_aim"] = p["s5_A_im"].reshape(1, g * pn)
    w["s5_ldt"] = jnp.repeat(p["s5_log_dt"], pn).reshape(1, g * pn)
    w["s5_br"] = jnp.einsum("gpc,gh->gchp", p["s5_B_re"], eye).reshape(g * S5_CH, g * pn).astype(BF16)
    w["s5_bi"] = jnp.einsum("gpc,gh->gchp", p["s5_B_im"], eye).reshape(g * S5_CH, g * pn).astype(BF16)
    w["s5_cr"] = jnp.einsum("gcp,gh->gphc", p["s5_C_re"], eye).reshape(g * pn, g * S5_CH).astype(BF16)
    w["s5_ci"] = jnp.einsum("gcp,gh->gphc", p["s5_C_im"], eye).reshape(g * pn, g * S5_CH).astype(BF16)
    w["s5_d"] = p["s5_D"].reshape(1, -1)
    w["s5_wglu"] = p["s5_w_glu"].astype(BF16)
    w["s5_go"] = p["s5_out_g"].reshape(1, -1)

    w["ml_cw"] = p["ml_conv_w"]
    w["ml_cb"] = p["ml_conv_b"].reshape(1, -1)
    gate_b = jnp.concatenate([p["ml_b_i"], p["ml_b_f"]])
    w["ml_bcol"] = jnp.zeros((1, 128), F32).at[0, MISC_IG:MISC_IG + 2 * ML_HEADS].set(gate_b)
    w["ml_brow"] = jnp.broadcast_to(gate_b[:, None], (2 * ML_HEADS, 128))
    w["ml_gn"] = p["ml_norm_g"].reshape(1, -1)

    wuq = p["mla_w_uq"].reshape(-1, MLA_HEADS, MLA_NOPE + MLA_ROPE)
    w["mla_gq"] = p["mla_q_norm_g"].reshape(1, -1)
    w["mla_wn"] = wuq[:, :, :MLA_NOPE].reshape(wuq.shape[0], -1).astype(BF16)
    w["mla_wr"] = wuq[:, :, MLA_NOPE:].reshape(wuq.shape[0], -1).astype(BF16)
    wuk = p["mla_w_uk"]
    kvl = wuk.shape[0]
    wuk_t = jnp.transpose(wuk, (1, 2, 0)).reshape(MLA_HEADS // 2, 2, MLA_NOPE, kvl)
    eye2 = jnp.eye(2, dtype=F32)
    w["mla_wuk"] = jnp.einsum("jand,ab->janbd", wuk_t, eye2).reshape(MLA_HEADS // 2, 2 * MLA_NOPE, 2 * kvl).astype(BF16)
    w["mla_gkv"] = p["mla_kv_norm_g"].reshape(1, -1)
    eye8 = jnp.eye(MLA_HEADS, dtype=F32)
    w["mla_wuv"] = jnp.einsum("chv,hg->hcgv", p["mla_w_uv"], eye8).reshape(MLA_HEADS, kvl, MLA_HEADS * MLA_V).astype(BF16)
    w["mla_go"] = p["mla_out_g"].reshape(1, -1)

    wo = p["w_out"].astype(BF16)
    w["w_out"] = [wo[:256], wo[256:512], wo[512:]]
    w["norm_ca_g"] = p["norm_ca_g"]
    w["ca_mem_g"] = p["ca_mem_g"]
    w["ca_wq"] = p["ca_w_q"].astype(BF16)
    w["ca_wkv"] = jnp.concatenate([p["ca_w_k"], p["ca_w_v"]], axis=1).astype(BF16)
    w["ca_wo"] = p["ca_w_o"].astype(BF16)
    w["norm_ffn_g"] = p["norm_ffn_g"]
    w["ffn_w1"] = p["ffn_w1"].astype(BF16)
    w["ffn_w2"] = p["ffn_w2"].astype(BF16)
    return w


def _rope_tables(pos, reps):
    half = MLA_ROPE // 2
    inv = ROPE_BASE ** (-jnp.arange(half, dtype=F32) * 2.0 / MLA_ROPE)
    ang = pos.astype(F32)[:, None] * inv[None, :]
    cos, sin = jnp.cos(ang), jnp.sin(ang)
    zero = jnp.zeros_like(sin)
    n = 128 // MLA_ROPE
    tabs = (jnp.tile(jnp.concatenate([cos, cos], 1), (reps, n)),
            jnp.tile(jnp.concatenate([zero, sin], 1), (reps, n)),
            jnp.tile(jnp.concatenate([-sin, zero], 1), (reps, n)))
    return tabs


def _layer(x2, b, t, w, tables, mem_k, mem_v, s5_h0, ml_state, conv_buf, paged, final_g, cfg):
    m = b * t
    qw = ML_HEADS * ML_HD
    z = _linear([x2], [w["w_in"]], gamma=w["norm_mix_g"], tm=cfg["tm"])
    z3 = z.reshape(b, t, N_IN_PAD)

    u_t = jnp.transpose(z3[:, :, COL_U:COL_U + 256], (1, 0, 2))
    y_s5_t, s_re, s_im = _s5(u_t, s5_h0[0], s5_h0[1], w, cfg["s5_tc"])
    y_s5 = jnp.transpose(y_s5_t, (1, 0, 2)).reshape(m, 256)

    gates_t = jnp.transpose(z3[:, :, COL_MISC + MISC_IG:COL_MISC + MISC_IG + 2 * ML_HEADS], (0, 2, 1))
    c0, n0, m0 = ml_state
    c0_aug = jnp.concatenate([c0, n0[..., None], jnp.zeros(c0.shape[:3] + (ML_HD - 1,), F32)], axis=-1)
    m0_b = jnp.zeros((b, 8, 128), F32).at[:, :ML_HEADS, :].set(jnp.broadcast_to(m0[:, :, None], (b, ML_HEADS, 128)))
    conv0 = jnp.zeros((b, 8, 2 * qw), F32).at[:, 8 - (ML_CONV - 1):, :].set(conv_buf)
    y_ml, c_aug, m_out = _mlstm(z3, gates_t, conv0, c0_aug, m0_b, w, cfg["ml_chunk"])
    y_ml = y_ml.reshape(m, qw)
    conv_new = jnp.concatenate([conv_buf, z3[:, :, COL_Q:COL_Q + 2 * qw]], axis=1)[:, t:, :]
    ml_c, ml_n, ml_m = c_aug[..., :ML_HD], c_aug[..., ML_HD], m_out[:, :ML_HEADS, 0]

    ql, qr, c_lat, kcat, k_rope = _mla_prep(z, tables, t, w, cfg["tm"])
    kvl = c_lat.shape[1]
    q4 = jnp.concatenate([ql.reshape(b, t, MLA_HEADS, kvl), qr.reshape(b, t, MLA_HEADS, MLA_ROPE),
                          jnp.zeros((b, t, MLA_HEADS, QK_PAD - kvl - MLA_ROPE), BF16)], axis=-1)
    q4 = jnp.transpose(q4, (0, 2, 1, 3))
    kcat3 = kcat.reshape(b, t, QK_PAD)
    if paged is None:
        y_mla = _mla_prompt(q4, kcat3, w, cfg["tq"])
    else:
        cache_c, cache_r, layer, page_table = paged
        y_mla = _mla_sample(q4.reshape(b, MLA_HEADS * t, QK_PAD).astype(F32), kcat3, cache_c, cache_r, layer,
                            page_table, w, cfg["pages"])
    y_mla = y_mla.reshape(m, MLA_HEADS * MLA_V)

    x2 = _linear([y_s5, y_ml, y_mla], w["w_out"], residual=x2, tm=cfg["tm"])

    qc = _linear([x2], [w["ca_wq"]], gamma=w["norm_ca_g"], tm=cfg["tm"])
    oc = _cross(qc.reshape(b, t, -1), mem_k, mem_v, cfg["ca_tt"])
    x2 = _linear([oc.reshape(m, -1)], [w["ca_wo"]], residual=x2, tm=cfg["tm"])

    x2 = _mlp(x2, w["norm_ffn_g"], w["ffn_w1"], w["ffn_w2"], w["norm_ffn_g"] if final_g is None else final_g,
              final_g is not None, tm=cfg["tm"])
    pn = S5_STATE
    states = (s_re.reshape(b, -1, pn), s_im.reshape(b, -1, pn), ml_c, ml_n, ml_m, conv_new,
              c_lat.reshape(b, t, kvl), k_rope.reshape(b, t, MLA_ROPE))
    return x2, states


def kernel(x_prompt, x_sample, state_ssm_re, state_ssm_im, state_mlstm_C, state_mlstm_n, state_mlstm_m,
           state_mlstm_conv, cache_kv_latent, cache_k_rope, cache_mem_k, cache_mem_v, page_table, mem_prompt,
           norm_mix_g, w_in, s5_A_re, s5_A_im, s5_log_dt, s5_B_re, s5_B_im, s5_C_re, s5_C_im, s5_D, s5_w_glu,
           s5_out_g, ml_conv_w, ml_conv_b, ml_b_i, ml_b_f, ml_norm_g, mla_q_norm_g, mla_w_uq, mla_kv_norm_g,
           mla_w_uk, mla_w_uv, mla_out_g, w_out, norm_ca_g, ca_mem_g, ca_w_q, ca_w_k, ca_w_v, ca_w_o,
           norm_ffn_g, ffn_w1, ffn_w2, final_norm_g):
    stacked = dict(norm_mix_g=norm_mix_g, w_in=w_in, s5_A_re=s5_A_re, s5_A_im=s5_A_im, s5_log_dt=s5_log_dt,
                   s5_B_re=s5_B_re, s5_B_im=s5_B_im, s5_C_re=s5_C_re, s5_C_im=s5_C_im, s5_D=s5_D,
                   s5_w_glu=s5_w_glu, s5_out_g=s5_out_g, ml_conv_w=ml_conv_w, ml_conv_b=ml_conv_b, ml_b_i=ml_b_i,
                   ml_b_f=ml_b_f, ml_norm_g=ml_norm_g, mla_q_norm_g=mla_q_norm_g, mla_w_uq=mla_w_uq,
                   mla_kv_norm_g=mla_kv_norm_g, mla_w_uk=mla_w_uk, mla_w_uv=mla_w_uv, mla_out_g=mla_out_g,
                   w_out=w_out, norm_ca_g=norm_ca_g, ca_mem_g=ca_mem_g, ca_w_q=ca_w_q, ca_w_k=ca_w_k,
                   ca_w_v=ca_w_v, ca_w_o=ca_w_o, norm_ffn_g=norm_ffn_g, ffn_w1=ffn_w1, ffn_w2=ffn_w2)
    depth = w_in.shape[0]
    layers = [_prep_layer({k: v[l] for k, v in stacked.items()}) for l in range(depth)]

    bp, tp, d = x_prompt.shape
    bs, ts, _ = x_sample.shape
    gp = s5_A_re.shape[1] * s5_A_re.shape[2]
    qw = ML_HEADS * ML_HD
    n_mem = mem_prompt.shape[1]
    past_len = page_table.shape[1] * cache_kv_latent.shape[2]

    cfg_p = dict(tm=512, s5_tc=64, ml_chunk=128, tq=256, ca_tt=512, pages=1)
    tab_p = _rope_tables(jnp.arange(tp, dtype=jnp.int32), 1)
    zero_s5 = (jnp.zeros((bp, gp), F32), jnp.zeros((bp, gp), F32))
    zero_ml = (jnp.zeros((bp, ML_HEADS, ML_HD, ML_HD), F32), jnp.zeros((bp, ML_HEADS, ML_HD), F32),
               jnp.zeros((bp, ML_HEADS), F32))
    zero_conv = jnp.zeros((bp, ML_CONV - 1, 2 * qw), F32)
    xp = x_prompt.reshape(bp * tp, d)
    mem2 = mem_prompt.reshape(bp * n_mem, d)
    p_states = []
    for l, w in enumerate(layers):
        mkv = _linear([mem2], [w["ca_wkv"]], gamma=w["ca_mem_g"], tm=512)
        wd = mkv.shape[1] // 2
        mk = mkv[:, :wd].reshape(bp, n_mem, wd)
        mv = mkv[:, wd:].reshape(bp, n_mem, wd)
        xp, st = _layer(xp, bp, tp, w, tab_p, mk, mv, zero_s5, zero_ml, zero_conv, None,
                        final_norm_g if l == depth - 1 else None, cfg_p)
        p_states.append(st + (mk.reshape(bp, n_mem, CA_HEADS, CA_HD), mv.reshape(bp, n_mem, CA_HEADS, CA_HD)))
    y_prompt = xp.reshape(bp, tp, d)
    p_out = [jnp.stack([s[i] for s in p_states]) for i in range(10)]

    cfg_s = dict(tm=512, s5_tc=ts, ml_chunk=ts, tq=ts, ca_tt=ts, pages=16)
    tab_s = _rope_tables(past_len + jnp.arange(ts, dtype=jnp.int32), bs)
    xs = x_sample.reshape(bs * ts, d)
    s_states = []
    for l, w in enumerate(layers):
        mk = cache_mem_k[l].reshape(bs, n_mem, -1)
        mv = cache_mem_v[l].reshape(bs, n_mem, -1)
        xs, st = _layer(xs, bs, ts, w, tab_s, mk, mv,
                        (state_ssm_re[l].reshape(bs, gp), state_ssm_im[l].reshape(bs, gp)),
                        (state_mlstm_C[l], state_mlstm_n[l], state_mlstm_m[l]), state_mlstm_conv[l],
                        (cache_kv_latent, cache_k_rope, l, page_table),
                        final_norm_g if l == depth - 1 else None, cfg_s)
        s_states.append(st)
    y_sample = xs.reshape(bs, ts, d)
    s_out = [jnp.stack([s[i] for s in s_states]) for i in range(8)]

    return (y_prompt, y_sample, *p_out, *s_out)
```

```python
import functools
import math

import jax
import jax.numpy as jnp
from jax import lax
from jax.experimental import pallas as pl
from jax.experimental.pallas import tpu as pltpu

F32 = jnp.float32
BF16 = jnp.bfloat16
NORM_EPS = 1e-6
ROPE_BASE = 10000.0

S5_CH = 16
S5_STATE = 64
ML_HEADS = 4
ML_HD = 64
ML_CONV = 4
MLA_HEADS = 8
MLA_NOPE = 64
MLA_ROPE = 32
MLA_V = 64
CA_HEADS = 4
CA_HD = 64
QK_PAD = 256

COL_U, COL_Q, COL_K, COL_V, COL_O, COL_CQ, COL_CKV, COL_MISC = 0, 256, 512, 768, 1024, 1280, 1536, 1664
N_IN_PAD = 1792
MISC_KR, MISC_IG, MISC_FG = 0, 32, 36

VMEM_LIMIT = 56 * 1024 * 1024


def _cparams(*sem):
    return pltpu.CompilerParams(dimension_semantics=sem, vmem_limit_bytes=VMEM_LIMIT)


def _rms(x, g):
    return x * lax.rsqrt(jnp.mean(x * x, axis=-1, keepdims=True) + NORM_EPS) * g


def _dot(a, b):
    return jnp.dot(a, b, preferred_element_type=F32)


def _dot_nt(a, b):
    return lax.dot_general(a, b, (((1,), (1,)), ((), ())), preferred_element_type=F32)


def _dot_tn(a, b):
    return lax.dot_general(a, b, (((0,), (0,)), ((), ())), preferred_element_type=F32)


def _const_spec(shape):
    nd = len(shape)
    return pl.BlockSpec(shape, lambda *_: (0,) * nd)


def _linear_body(*refs, n_in, has_norm, has_res):
    x_refs = refs[:n_in]
    pos = n_in
    g_ref = refs[pos] if has_norm else None
    pos += int(has_norm)
    w_refs = refs[pos:pos + n_in]
    pos += n_in
    res_ref = refs[pos] if has_res else None
    pos += int(has_res)
    o_ref = refs[pos]
    acc = None
    for x_ref, w_ref in zip(x_refs, w_refs):
        x = x_ref[...]
        if has_norm:
            x = _rms(x, g_ref[...])
        p = _dot(x.astype(BF16), w_ref[...])
        acc = p if acc is None else acc + p
    if has_res:
        acc = acc + res_ref[...]
    o_ref[...] = acc


def _linear(xs, ws, gamma=None, residual=None, tm=512):
    m = xs[0].shape[0]
    n = ws[0].shape[1]
    tm = min(tm, m)
    assert m % tm == 0
    in_specs = [pl.BlockSpec((tm, x.shape[1]), lambda i: (i, 0)) for x in xs]
    args = list(xs)
    if gamma is not None:
        in_specs.append(_const_spec((1, xs[0].shape[1])))
        args.append(gamma.reshape(1, -1))
    for w in ws:
        in_specs.append(_const_spec(w.shape))
        args.append(w)
    if residual is not None:
        in_specs.append(pl.BlockSpec((tm, n), lambda i: (i, 0)))
        args.append(residual)
    body = functools.partial(_linear_body, n_in=len(xs), has_norm=gamma is not None, has_res=residual is not None)
    return pl.pallas_call(
        body,
        grid=(m // tm,),
        in_specs=in_specs,
        out_specs=pl.BlockSpec((tm, n), lambda i: (i, 0)),
        out_shape=jax.ShapeDtypeStruct((m, n), F32),
        compiler_params=_cparams("parallel"),
        name="linear",
    )(*args)


def _mlp_body(x_ref, g_ref, w1_ref, w2_ref, gf_ref, o_ref, *, fc, final_norm):
    x = x_ref[...]
    h = _rms(x, g_ref[...]).astype(BF16)
    acc = x
    for c in range(w1_ref.shape[1] // fc):
        a = _dot(h, w1_ref[:, c * fc:(c + 1) * fc])
        a = jnp.maximum(a, 0.0)
        acc = acc + _dot((a * a).astype(BF16), w2_ref[c * fc:(c + 1) * fc, :])
    if final_norm:
        acc = _rms(acc, gf_ref[...])
    o_ref[...] = acc


def _mlp(x, gamma, w1, w2, final_gamma, final_norm, tm=512, fc=1024):
    m, d = x.shape
    dff = w1.shape[1]
    tm = min(tm, m)
    fc = min(fc, dff)
    body = functools.partial(_mlp_body, fc=fc, final_norm=final_norm)
    return pl.pallas_call(
        body,
        grid=(m // tm,),
        in_specs=[
            pl.BlockSpec((tm, d), lambda i: (i, 0)),
            _const_spec((1, d)),
            pl.BlockSpec((d, dff), lambda i: (0, 0), pipeline_mode=pl.Buffered(1)),
            pl.BlockSpec((dff, d), lambda i: (0, 0), pipeline_mode=pl.Buffered(1)),
            _const_spec((1, d)),
        ],
        out_specs=pl.BlockSpec((tm, d), lambda i: (i, 0)),
        out_shape=jax.ShapeDtypeStruct((m, d), F32),
        compiler_params=_cparams("parallel"),
        name="mlp",
    )(x, gamma.reshape(1, -1), w1, w2, final_gamma.reshape(1, -1))


def _s5_body(u_ref, h0r_ref, h0i_ref, are_ref, aim_ref, ldt_ref, br_ref, bi_ref, cr_ref, ci_ref, d_ref, wg_ref,
             go_ref, y_ref, hro_ref, hio_ref, xr_s, xi_s, hr_s, hi_s, *, tc, bb):
    @pl.when(pl.program_id(1) == 0)
    def _():
        hr_s[...] = h0r_ref[...]
        hi_s[...] = h0i_ref[...]

    ar = are_ref[...]
    ai = aim_ref[...]
    dt = jnp.exp(ldt_ref[...])
    mag = jnp.exp(ar * dt)
    lr = mag * jnp.cos(ai * dt)
    li = mag * jnp.sin(ai * dt)
    den = ar * ar + ai * ai
    zr = lr - 1.0
    fr = (zr * ar + li * ai) / den
    fi = (li * ar - zr * ai) / den

    ch = u_ref.shape[2]
    u = u_ref[...].reshape(tc * bb, ch)
    ub = u.astype(BF16)
    pr = _dot(ub, br_ref[...])
    pi = _dot(ub, bi_ref[...])
    xr_s[...] = fr * pr - fi * pi
    xi_s[...] = fr * pi + fi * pr

    def step(t, carry):
        hr, hi = carry
        rows = pl.ds(pl.multiple_of(t * bb, bb), bb)
        nr = lr * hr - li * hi + xr_s[rows, :]
        ni = lr * hi + li * hr + xi_s[rows, :]
        xr_s[rows, :] = nr
        xi_s[rows, :] = ni
        return nr, ni

    hr, hi = lax.fori_loop(0, tc, step, (hr_s[...], hi_s[...]))
    hr_s[...] = hr
    hi_s[...] = hi
    hro_ref[...] = hr
    hio_ref[...] = hi

    y = _dot(xr_s[...].astype(BF16), cr_ref[...]) - _dot(xi_s[...].astype(BF16), ci_ref[...])
    y = jax.nn.gelu(y + d_ref[...] * u)
    g = _dot(y.astype(BF16), wg_ref[...])
    o = g[:, :ch] * jax.nn.sigmoid(g[:, ch:])
    y_ref[...] = _rms(o, go_ref[...]).reshape(tc, bb, ch)


def _s5(u_t, h0r, h0i, w, tc):
    t, b, ch = u_t.shape
    gp = h0r.shape[1]
    bb = b if b <= 128 else 128
    tc = min(tc, t)
    body = functools.partial(_s5_body, tc=tc, bb=bb)
    row = lambda n: _const_spec((1, n))
    return pl.pallas_call(
        body,
        grid=(b // bb, t // tc),
        in_specs=[
            pl.BlockSpec((tc, bb, ch), lambda i, j: (j, i, 0)),
            pl.BlockSpec((bb, gp), lambda i, j: (i, 0)),
            pl.BlockSpec((bb, gp), lambda i, j: (i, 0)),
            row(gp), row(gp), row(gp),
            _const_spec((ch, gp)), _const_spec((ch, gp)),
            _const_spec((gp, ch)), _const_spec((gp, ch)),
            row(ch), _const_spec((ch, 2 * ch)), row(ch),
        ],
        out_specs=[
            pl.BlockSpec((tc, bb, ch), lambda i, j: (j, i, 0)),
            pl.BlockSpec((bb, gp), lambda i, j: (i, 0)),
            pl.BlockSpec((bb, gp), lambda i, j: (i, 0)),
        ],
        out_shape=[
            jax.ShapeDtypeStruct((t, b, ch), F32),
            jax.ShapeDtypeStruct((b, gp), F32),
            jax.ShapeDtypeStruct((b, gp), F32),
        ],
        scratch_shapes=[
            pltpu.VMEM((tc * bb, gp), F32), pltpu.VMEM((tc * bb, gp), F32),
            pltpu.VMEM((bb, gp), F32), pltpu.VMEM((bb, gp), F32),
        ],
        compiler_params=_cparams("parallel", "arbitrary"),
        name="s5",
    )(u_t, h0r, h0i, w["s5_are"], w["s5_aim"], w["s5_ldt"], w["s5_br"], w["s5_bi"], w["s5_cr"], w["s5_ci"],
      w["s5_d"], w["s5_wglu"], w["s5_go"])


def _split3(x):
    hi = x.astype(BF16)
    r1 = x - hi.astype(F32)
    mid = r1.astype(BF16)
    lo = (r1 - mid.astype(F32)).astype(BF16)
    return hi, mid, lo


def _mlstm_body(zq_ref, zk_ref, zv_ref, zo_ref, misc_ref, gt_ref, conv0_ref, c0_ref, m0_ref, cw_ref, cb_ref,
                bcol_ref, brow_ref, gn_ref, y_ref, co_ref, mo_ref, padq, padk, cs, ms, *, cl):
    hd = ML_HD
    qw = ML_HEADS * hd

    @pl.when(pl.program_id(1) == 0)
    def _():
        padq[0:8, :] = conv0_ref[0][:, :qw]
        padk[0:8, :] = conv0_ref[0][:, qw:]
        cs[...] = c0_ref[0]
        ms[...] = m0_ref[0]

    padq[8:8 + cl, :] = zq_ref[0]
    padk[8:8 + cl, :] = zk_ref[0]
    cw = cw_ref[...]
    cb = cb_ref[...]

    def conv(pad, w, b):
        y = b
        for j in range(ML_CONV):
            y = y + pad[8 - (ML_CONV - 1) + j:8 - (ML_CONV - 1) + j + cl, :] * w[j:j + 1, :]
        return y

    q = jax.nn.silu(conv(padq, cw[:, :qw], cb[:, :qw]))
    k = jax.nn.silu(conv(padk, cw[:, qw:], cb[:, qw:])) * (hd ** -0.5)
    if cl >= 8:
        padq[0:8, :] = padq[cl:cl + 8, :]
        padk[0:8, :] = padk[cl:cl + 8, :]

    gcol = misc_ref[0] + bcol_ref[...]
    lf_col = jax.nn.log_sigmoid(gcol)
    grow = gt_ref[0] + brow_ref[:, 0:1]
    ig_row = grow[0:ML_HEADS]
    lf_row = jax.nn.log_sigmoid(grow[ML_HEADS:2 * ML_HEADS])

    ri = lax.broadcasted_iota(jnp.int32, (cl, cl), 0)
    ci = lax.broadcasted_iota(jnp.int32, (cl, cl), 1)
    tril = ri >= ci
    lower = jnp.where(tril, 1.0, 0.0).astype(BF16)
    upper = jnp.where(ri <= ci, 1.0, 0.0).astype(BF16)
    bc_col = sum(_dot(lower, part) for part in _split3(lf_col))
    bc_row = sum(_dot(part, upper) for part in _split3(lf_row))

    v = zv_ref[0]
    og = jax.nn.sigmoid(zo_ref[0])
    gn = gn_ref[...]
    one_col = jnp.where(lax.broadcasted_iota(jnp.int32, (cl, hd), 1) == 0, 1.0, 0.0)
    outs = []
    for h in range(ML_HEADS):
        sl = slice(h * hd, (h + 1) * hd)
        qh = q[:, sl].astype(BF16)
        kh = k[:, sl]
        v_aug = jnp.concatenate([v[:, sl], one_col], axis=1).astype(BF16)
        bcj = bc_col[:, MISC_FG + h:MISC_FG + h + 1]
        igj = gcol[:, MISC_IG + h:MISC_IG + h + 1]
        m_prev = ms[h:h + 1, 0:1]
        d = bcj - bc_row[h:h + 1, :] + ig_row[h:h + 1, :]
        d = jnp.where(tril, d, -jnp.inf)
        inter = bcj + m_prev
        m_tok = jnp.maximum(inter, jnp.max(d, axis=1, keepdims=True))
        w_intra = jnp.exp(d - m_tok)
        w_inter = jnp.exp(inter - m_tok)
        s = _dot_nt(qh, kh.astype(BF16)) * w_intra
        c_aug = cs[h]
        nd = _dot(s.astype(BF16), v_aug) + w_inter * _dot(qh, c_aug.astype(BF16))
        num = nd[:, :hd]
        den = nd[:, hd:hd + 1]
        hh = num / jnp.maximum(jnp.abs(den), jnp.exp(-m_tok))
        m_end = m_tok[cl - 1:cl, :]
        g_end = bcj[cl - 1:cl, :]
        w_s = jnp.exp(g_end - bcj + igj - m_end)
        decay = jnp.exp(g_end + m_prev - m_end)
        cs[h] = decay * c_aug + _dot_tn((kh * w_s).astype(BF16), v_aug)
        ms[h:h + 1, :] = jnp.broadcast_to(m_end, (1, ms.shape[1]))
        oh = og[:, sl] * hh
        outs.append(oh * lax.rsqrt(jnp.mean(oh * oh, axis=-1, keepdims=True) + NORM_EPS) * gn[:, sl])
    y_ref[0] = jnp.concatenate(outs, axis=1)
    co_ref[0] = cs[...]
    mo_ref[0] = ms[...]


def _mlstm(z3, gates_t, conv0, c0_aug, m0, w, cl):
    b, t, _ = z3.shape
    qw = ML_HEADS * ML_HD
    cl = min(cl, t)
    body = functools.partial(_mlstm_body, cl=cl)
    zspec = lambda col: pl.BlockSpec((1, cl, qw), lambda i, j: (i, j, col // qw))
    return pl.pallas_call(
        body,
        grid=(b, t // cl),
        in_specs=[
            zspec(COL_Q), zspec(COL_K), zspec(COL_V), zspec(COL_O),
            pl.BlockSpec((1, cl, 128), lambda i, j: (i, j, COL_MISC // 128)),
            pl.BlockSpec((1, 8, cl), lambda i, j: (i, 0, j)),
            pl.BlockSpec((1, 8, 2 * qw), lambda i, j: (i, 0, 0)),
            pl.BlockSpec((1, ML_HEADS, ML_HD, 2 * ML_HD), lambda i, j: (i, 0, 0, 0)),
            pl.BlockSpec((1, 8, 128), lambda i, j: (i, 0, 0)),
            _const_spec((ML_CONV, 2 * qw)), _const_spec((1, 2 * qw)),
            _const_spec((1, 128)), _const_spec((8, 128)), _const_spec((1, qw)),
        ],
        out_specs=[
            pl.BlockSpec((1, cl, qw), lambda i, j: (i, j, 0)),
            pl.BlockSpec((1, ML_HEADS, ML_HD, 2 * ML_HD), lambda i, j: (i, 0, 0, 0)),
            pl.BlockSpec((1, 8, 128), lambda i, j: (i, 0, 0)),
        ],
        out_shape=[
            jax.ShapeDtypeStruct((b, t, qw), F32),
            jax.ShapeDtypeStruct((b, ML_HEADS, ML_HD, 2 * ML_HD), F32),
            jax.ShapeDtypeStruct((b, 8, 128), F32),
        ],
        scratch_shapes=[
            pltpu.VMEM((cl + 8, qw), F32), pltpu.VMEM((cl + 8, qw), F32),
            pltpu.VMEM((ML_HEADS, ML_HD, 2 * ML_HD), F32), pltpu.VMEM((8, 128), F32),
        ],
        compiler_params=_cparams("parallel", "arbitrary"),
        name="mlstm",
    )(z3, z3, z3, z3, z3, gates_t, conv0, c0_aug, m0, w["ml_cw"], w["ml_cb"], w["ml_bcol"], w["ml_brow"], w["ml_gn"])


def _rope128(x, cos_t, sin_up, sin_dn):
    half = MLA_ROPE // 2
    return x * cos_t + pltpu.roll(x, half, 1) * sin_up + pltpu.roll(x, 128 - half, 1) * sin_dn


def _mla_prep_body(zcq_ref, zckv_ref, misc_ref, cos_ref, sup_ref, sdn_ref, gq_ref, wn_ref, wr_ref, wuk_ref, gkv_ref,
                   sel_ref, *out_refs, heads_major):
    cos_t, sup, sdn = cos_ref[...], sup_ref[...], sdn_ref[...]
    cq = _rms(zcq_ref[...], gq_ref[...]).astype(BF16)
    qn = _dot(cq, wn_ref[...])
    qrp = _dot(cq, wr_ref[...])
    ql = [_dot(qn[:, 128 * j:128 * (j + 1)].astype(BF16), wuk_ref[j]).astype(BF16) for j in range(wuk_ref.shape[0])]
    qr = [_rope128(qrp[:, 128 * j:128 * (j + 1)], cos_t, sup, sdn).astype(BF16) for j in range(qrp.shape[1] // 128)]
    if heads_major:
        q_ref, c_ref, kcat_ref, kr_ref = out_refs
        kvl = ql[0].shape[1] // 2
        qr_all = jnp.concatenate(qr, axis=1)
        for h in range(MLA_HEADS):
            q_ref[0, h, :, 0:kvl] = ql[h // 2][:, (h % 2) * kvl:(h % 2 + 1) * kvl]
            q_ref[0, h, :, kvl:QK_PAD] = _dot(qr_all, sel_ref[h]).astype(BF16)
    else:
        ql_ref, qr_ref, c_ref, kcat_ref, kr_ref = out_refs
        for j, v in enumerate(ql):
            ql_ref[:, 256 * j:256 * (j + 1)] = v
        for j, v in enumerate(qr):
            qr_ref[:, 128 * j:128 * (j + 1)] = v
    c = _rms(zckv_ref[...], gkv_ref[...])
    c_ref[...] = c
    krf = _rope128(misc_ref[...], cos_t, sup, sdn)
    kr_ref[...] = krf[:, MISC_KR:MISC_KR + MLA_ROPE]
    lane = lax.broadcasted_iota(jnp.int32, krf.shape, 1)
    kcat_ref[:, 0:128] = c.astype(BF16)
    kcat_ref[:, 128:256] = jnp.where(lane < MLA_ROPE, krf, 0.0).astype(BF16)


def _mla_prep(z, tables, b, t_len, w, tm, heads_major):
    m = z.shape[0]
    tm = min(tm, m, t_len) if heads_major else m
    nt = tables[0].shape[0] // tm
    tspec = pl.BlockSpec((tm, 128), lambda i: (i % nt, 0))
    kvl = 128
    if heads_major:
        q_specs = [pl.BlockSpec((1, MLA_HEADS, tm, QK_PAD), lambda i: (i // nt, 0, i % nt, 0))]
        q_shapes = [jax.ShapeDtypeStruct((b, MLA_HEADS, t_len, QK_PAD), BF16)]
    else:
        q_specs = [pl.BlockSpec((tm, MLA_HEADS * kvl), lambda i: (i, 0)),
                   pl.BlockSpec((tm, MLA_HEADS * MLA_ROPE), lambda i: (i, 0))]
        q_shapes = [jax.ShapeDtypeStruct((m, MLA_HEADS * kvl), BF16),
                    jax.ShapeDtypeStruct((m, MLA_HEADS * MLA_ROPE), BF16)]
    return pl.pallas_call(
        functools.partial(_mla_prep_body, heads_major=heads_major),
        grid=(m // tm,),
        in_specs=[
            pl.BlockSpec((tm, 256), lambda i: (i, COL_CQ // 256)),
            pl.BlockSpec((tm, kvl), lambda i: (i, COL_CKV // 128)),
            pl.BlockSpec((tm, 128), lambda i: (i, COL_MISC // 128)),
            tspec, tspec, tspec,
            _const_spec((1, 256)), _const_spec(w["mla_wn"].shape), _const_spec(w["mla_wr"].shape),
            _const_spec(w["mla_wuk"].shape), _const_spec((1, kvl)), _const_spec(w["mla_sel"].shape),
        ],
        out_specs=q_specs + [
            pl.BlockSpec((tm, kvl), lambda i: (i, 0)),
            pl.BlockSpec((tm, QK_PAD), lambda i: (i, 0)),
            pl.BlockSpec((tm, MLA_ROPE), lambda i: (i, 0)),
        ],
        out_shape=q_shapes + [
            jax.ShapeDtypeStruct((m, kvl), F32),
            jax.ShapeDtypeStruct((m, QK_PAD), BF16),
            jax.ShapeDtypeStruct((m, MLA_ROPE), F32),
        ],
        compiler_params=_cparams("parallel"),
        name="mla_prep",
    )(z, z, z, tables[0], tables[1], tables[2], w["mla_gq"], w["mla_wn"], w["mla_wr"], w["mla_wuk"], w["mla_gkv"],
      w["mla_sel"])


def _mla_out(o, tq, wuv_ref, g_ref):
    ob = o.astype(BF16)
    y = None
    for h in range(MLA_HEADS):
        p = _dot(ob[h * tq:(h + 1) * tq], wuv_ref[h])
        y = p if y is None else y + p
    return _rms(y, g_ref[...])


def _mla_prompt_body(q_ref, k_ref, wuv_ref, g_ref, o_ref, m_s, l_s, acc_s, *, tq, scale):
    i = pl.program_id(1)
    kvl = acc_s.shape[1]
    m_s[...] = jnp.full(m_s.shape, -jnp.inf, F32)
    l_s[...] = jnp.zeros(l_s.shape, F32)
    acc_s[...] = jnp.zeros(acc_s.shape, F32)

    def block(j, masked):
        kb = k_ref[0, pl.ds(pl.multiple_of(j * tq, tq), tq), :]
        vb = kb[:, :kvl]
        if masked:
            causal = (lax.broadcasted_iota(jnp.int32, (tq, tq), 1) <= lax.broadcasted_iota(jnp.int32, (tq, tq), 0))
        for h in range(MLA_HEADS):
            rows = slice(h * tq, (h + 1) * tq)
            s = _dot_nt(q_ref[0, h], kb) * scale
            if masked:
                s = jnp.where(causal, s, -jnp.inf)
            m_prev = m_s[rows, :]
            m_next = jnp.maximum(m_prev, jnp.max(s, axis=1, keepdims=True))
            alpha = jnp.exp(m_prev - m_next)
            p = jnp.exp(s - jnp.concatenate([m_next] * (tq // kvl), axis=1))
            p_lanes = p[:, 0:kvl]
            for c in range(1, tq // kvl):
                p_lanes = p_lanes + p[:, c * kvl:(c + 1) * kvl]
            l_s[rows, :] = alpha * l_s[rows, :] + p_lanes
            acc_s[rows, :] = alpha * acc_s[rows, :] + _dot(p.astype(BF16), vb)
            m_s[rows, :] = m_next

    def full_block(j, carry):
        block(j, False)
        return carry

    lax.fori_loop(0, i, full_block, 0)
    block(i, True)
    o_ref[0] = _mla_out(acc_s[...] / jnp.sum(l_s[...], axis=1, keepdims=True), tq, wuv_ref, g_ref)


def _mla_prompt(q, kcat, w, tq):
    b, hn, t, qk = q.shape
    tq = min(tq, t)
    kvl = 128
    wo = MLA_HEADS * MLA_V
    body = functools.partial(_mla_prompt_body, tq=tq, scale=1.0 / math.sqrt(MLA_NOPE + MLA_ROPE))
    return pl.pallas_call(
        body,
        grid=(b, t // tq),
        in_specs=[
            pl.BlockSpec((1, hn, tq, qk), lambda i, j: (i, 0, j, 0)),
            pl.BlockSpec((1, t, qk), lambda i, j: (i, 0, 0)),
            _const_spec(w["mla_wuv"].shape), _const_spec((1, wo)),
        ],
        out_specs=pl.BlockSpec((1, tq, wo), lambda i, j: (i, j, 0)),
        out_shape=jax.ShapeDtypeStruct((b, t, wo), F32),
        scratch_shapes=[pltpu.VMEM((hn * tq, kvl), F32), pltpu.VMEM((hn * tq, kvl), F32),
                        pltpu.VMEM((hn * tq, kvl), F32)],
        compiler_params=_cparams("parallel", "parallel"),
        name="mla_prompt",
    )(q, kcat, w["mla_wuv"], w["mla_go"])


def _mla_sample_body(pt_ref, q_ref, kn_ref, cc_hbm, cr_hbm, wuv_ref, g_ref, o_ref, cbuf, rbuf, sem, m_s, l_s, acc_s,
                     *, layer, group, n_groups, t_new, scale):
    b = pl.program_id(0)
    kvl = acc_s.shape[1]
    page = cbuf.shape[1] // group

    def group_copies(bi, g, slot):
        cps = []
        for i in range(group):
            pid = pt_ref[bi, g * group + i]
            cps.append(pltpu.make_async_copy(cc_hbm.at[layer, pid], cbuf.at[slot, pl.ds(i * page, page), :],
                                             sem.at[slot]))
            cps.append(pltpu.make_async_copy(cr_hbm.at[layer, pid], rbuf.at[slot, :, pl.ds(i * page, page)],
                                             sem.at[slot]))
        return cps

    def start_group(bi, g, slot):
        for cp in group_copies(bi, g, slot):
            cp.start()

    def wait_group(bi, g, slot):
        for cp in group_copies(bi, g, slot):
            cp.wait()

    @pl.when(b == 0)
    def _():
        start_group(0, 0, 0)

    m_s[...] = jnp.full(m_s.shape, -jnp.inf, F32)
    l_s[...] = jnp.zeros(l_s.shape, F32)
    acc_s[...] = jnp.zeros(acc_s.shape, F32)
    q = q_ref[0].astype(BF16)
    q_lat = q[:, 0:kvl]
    q_rope = q[:, kvl:kvl + MLA_ROPE]

    first = lax.rem(b * n_groups, 2)
    for g in range(n_groups):
        slot = lax.rem(first + g, 2)
        if g + 1 < n_groups:
            start_group(b, g + 1, 1 - slot)
        else:
            @pl.when(b + 1 < pl.num_programs(0))
            def _():
                start_group(b + 1, 0, 1 - slot)
        wait_group(b, g, slot)
        cb = cbuf[slot].astype(BF16)
        rb = rbuf[slot].astype(BF16)
        s = (_dot_nt(q_lat, cb) + _dot(q_rope, rb)) * scale
        m_old = m_s[...]
        m_new = jnp.maximum(m_old, jnp.max(s, axis=1, keepdims=True))
        alpha = jnp.exp(m_old - m_new)
        p = jnp.exp(s - m_new)
        l_s[...] = alpha * l_s[...] + jnp.sum(p, axis=1, keepdims=True)
        acc_s[...] = alpha * acc_s[...] + _dot(p.astype(BF16), cb)
        m_s[...] = m_new

    kn = kn_ref[0].astype(F32)
    qf = q.astype(F32)
    t_row = lax.broadcasted_iota(jnp.int32, (q.shape[0], 1), 0) % t_new
    s_new = []
    for t2 in range(t_new):
        st = jnp.sum(qf * kn[t2:t2 + 1, :], axis=1, keepdims=True) * scale
        s_new.append(jnp.where(t_row >= t2, st, -jnp.inf))
    m_o = m_s[...]
    m_n = m_o
    for st in s_new:
        m_n = jnp.maximum(m_n, st)
    al = jnp.exp(m_o - m_n)
    l = al * l_s[...]
    acc = al * acc_s[...]
    for t2, st in enumerate(s_new):
        pt = jnp.exp(st - m_n)
        l = l + pt
        acc = acc + pt * kn[t2:t2 + 1, 0:kvl]
    acc_s[...] = acc / l
    y = None
    for h in range(MLA_HEADS):
        ph = _dot(acc_s[h * t_new:(h + 1) * t_new, :].astype(BF16), wuv_ref[h])
        y = ph if y is None else y + ph
    o_ref[0] = _rms(y, g_ref[...])


def _mla_sample(q, knew, cache_c, cache_r, layer, page_table, w, pages):
    b, rows, qk = q.shape
    t_new = knew.shape[1]
    n_pages = page_table.shape[1]
    page, kvl = cache_c.shape[2], cache_c.shape[3]
    group = min(pages, n_pages)
    assert n_pages % group == 0
    wo = MLA_HEADS * MLA_V
    body = functools.partial(_mla_sample_body, layer=layer, group=group, n_groups=n_pages // group, t_new=t_new,
                             scale=1.0 / math.sqrt(MLA_NOPE + MLA_ROPE))
    grid_spec = pltpu.PrefetchScalarGridSpec(
        num_scalar_prefetch=1,
        grid=(b,),
        in_specs=[
            pl.BlockSpec((1, rows, qk), lambda bi, pt: (bi, 0, 0)),
            pl.BlockSpec((1, t_new, qk), lambda bi, pt: (bi, 0, 0)),
            pl.BlockSpec(memory_space=pl.ANY),
            pl.BlockSpec(memory_space=pl.ANY),
            pl.BlockSpec(w["mla_wuv"].shape, lambda bi, pt: (0, 0, 0)),
            pl.BlockSpec((1, wo), lambda bi, pt: (0, 0)),
        ],
        out_specs=pl.BlockSpec((1, t_new, wo), lambda bi, pt: (bi, 0, 0)),
        scratch_shapes=[
            pltpu.VMEM((2, group * page, kvl), F32),
            pltpu.VMEM((2, MLA_ROPE, group * page), F32),
            pltpu.SemaphoreType.DMA((2,)),
            pltpu.VMEM((rows, 1), F32), pltpu.VMEM((rows, 1), F32), pltpu.VMEM((rows, kvl), F32),
        ],
    )
    return pl.pallas_call(
        body,
        grid_spec=grid_spec,
        out_shape=jax.ShapeDtypeStruct((b, t_new, wo), F32),
        compiler_params=_cparams("arbitrary"),
        name="mla_sample",
    )(page_table, q, knew, cache_c, cache_r, w["mla_wuv"], w["mla_go"])


def _cross_body(q_ref, k_ref, v_ref, o_ref, *, kv_t):
    q = q_ref[0]
    kb = k_ref[...].reshape(k_ref.shape[-2:]).astype(BF16)
    vb = v_ref[...].reshape(v_ref.shape[-2:]).astype(BF16)
    lane = lax.broadcasted_iota(jnp.int32, q.shape, 1)
    out = jnp.zeros(q.shape, F32)
    for h in range(CA_HEADS):
        sel = (lane >= h * CA_HD) & (lane < (h + 1) * CA_HD)
        qh = jnp.where(sel, q, 0.0).astype(BF16)
        s = (_dot(qh, kb) if kv_t else _dot_nt(qh, kb)) * (CA_HD ** -0.5)
        e = jnp.exp(s - jnp.max(s, axis=1, keepdims=True))
        p = (e / jnp.sum(e, axis=1, keepdims=True)).astype(BF16)
        out = out + jnp.where(sel, _dot_nt(p, vb) if kv_t else _dot(p, vb), 0.0)
    o_ref[0] = out


def _cross(q3, mem_k, mem_v, tt, kv_t, layer=None):
    b, t, wd = q3.shape
    nm = mem_k.shape[-1] if kv_t else mem_k.shape[-2]
    tt = min(tt, t)
    kv_block = (1, wd, nm) if kv_t else (1, nm, wd)
    if layer is None:
        kv_spec = pl.BlockSpec(kv_block, lambda i, j: (i, 0, 0))
    else:
        kv_spec = pl.BlockSpec((1,) + kv_block, lambda i, j: (layer, i, 0, 0))
    return pl.pallas_call(
        functools.partial(_cross_body, kv_t=kv_t),
        grid=(b, t // tt),
        in_specs=[pl.BlockSpec((1, tt, wd), lambda i, j: (i, j, 0)), kv_spec, kv_spec],
        out_specs=pl.BlockSpec((1, tt, wd), lambda i, j: (i, j, 0)),
        out_shape=jax.ShapeDtypeStruct((b, t, wd), F32),
        compiler_params=_cparams("parallel", "parallel"),
        name="cross_attn",
    )(q3, mem_k, mem_v)


def _prep_layer(p):
    w = {}
    wi = p["w_in"]
    d = wi.shape[0]
    o_i = 4 * 256
    w["w_in"] = jnp.concatenate(
        [wi[:, :o_i], wi[:, o_i + 8:o_i + 8 + 256 + 256 + 128 + 32], wi[:, o_i:o_i + 8],
         jnp.zeros((d, N_IN_PAD - wi.shape[1]), wi.dtype)], axis=1).astype(BF16)
    w["norm_mix_g"] = p["norm_mix_g"]

    g, pn = p["s5_A_re"].shape
    eye = jnp.eye(g, dtype=F32)
    w["s5_are"] = p["s5_A_re"].reshape(1, g * pn)
    w["s5_aim"] = p["s5_A_im"].reshape(1, g * pn)
    w["s5_ldt"] = jnp.repeat(p["s5_log_dt"], pn).reshape(1, g * pn)
    w["s5_br"] = jnp.einsum("gpc,gh->gchp", p["s5_B_re"], eye).reshape(g * S5_CH, g * pn).astype(BF16)
    w["s5_bi"] = jnp.einsum("gpc,gh->gchp", p["s5_B_im"], eye).reshape(g * S5_CH, g * pn).astype(BF16)
    w["s5_cr"] = jnp.einsum("gcp,gh->gphc", p["s5_C_re"], eye).reshape(g * pn, g * S5_CH).astype(BF16)
    w["s5_ci"] = jnp.einsum("gcp,gh->gphc", p["s5_C_im"], eye).reshape(g * pn, g * S5_CH).astype(BF16)
    w["s5_d"] = p["s5_D"].reshape(1, -1)
    w["s5_wglu"] = p["s5_w_glu"].astype(BF16)
    w["s5_go"] = p["s5_out_g"].reshape(1, -1)

    w["ml_cw"] = p["ml_conv_w"]
    w["ml_cb"] = p["ml_conv_b"].reshape(1, -1)
    gate_b = jnp.concatenate([p["ml_b_i"], p["ml_b_f"]])
    w["ml_bcol"] = jnp.zeros((1, 128), F32).at[0, MISC_IG:MISC_IG + 2 * ML_HEADS].set(gate_b)
    w["ml_brow"] = jnp.broadcast_to(gate_b[:, None], (2 * ML_HEADS, 128))
    w["ml_gn"] = p["ml_norm_g"].reshape(1, -1)

    wuq = p["mla_w_uq"].reshape(-1, MLA_HEADS, MLA_NOPE + MLA_ROPE)
    w["mla_gq"] = p["mla_q_norm_g"].reshape(1, -1)
    w["mla_wn"] = wuq[:, :, :MLA_NOPE].reshape(wuq.shape[0], -1).astype(BF16)
    w["mla_wr"] = wuq[:, :, MLA_NOPE:].reshape(wuq.shape[0], -1).astype(BF16)
    wuk = p["mla_w_uk"]
    kvl = wuk.shape[0]
    wuk_t = jnp.transpose(wuk, (1, 2, 0)).reshape(MLA_HEADS // 2, 2, MLA_NOPE, kvl)
    eye2 = jnp.eye(2, dtype=F32)
    w["mla_wuk"] = jnp.einsum("jand,ab->janbd", wuk_t, eye2).reshape(MLA_HEADS // 2, 2 * MLA_NOPE, 2 * kvl).astype(BF16)
    w["mla_gkv"] = p["mla_kv_norm_g"].reshape(1, -1)
    src = jnp.arange(MLA_HEADS * MLA_ROPE)
    w["mla_sel"] = ((src[None, :, None] // MLA_ROPE == jnp.arange(MLA_HEADS)[:, None, None])
                    & (src[None, :, None] % MLA_ROPE == jnp.arange(128)[None, None, :])).astype(BF16)
    eye8 = jnp.eye(MLA_HEADS, dtype=F32)
    w["mla_wuv"] = jnp.einsum("chv,hg->hcgv", p["mla_w_uv"], eye8).reshape(MLA_HEADS, kvl, MLA_HEADS * MLA_V).astype(BF16)
    w["mla_go"] = p["mla_out_g"].reshape(1, -1)

    wo = p["w_out"].astype(BF16)
    w["w_out"] = [wo[:256], wo[256:512], wo[512:]]
    w["norm_ca_g"] = p["norm_ca_g"]
    w["ca_mem_g"] = p["ca_mem_g"]
    w["ca_wq"] = p["ca_w_q"].astype(BF16)
    w["ca_wkv"] = jnp.concatenate([p["ca_w_k"], p["ca_w_v"]], axis=1).astype(BF16)
    w["ca_wo"] = p["ca_w_o"].astype(BF16)
    w["norm_ffn_g"] = p["norm_ffn_g"]
    w["ffn_w1"] = p["ffn_w1"].astype(BF16)
    w["ffn_w2"] = p["ffn_w2"].astype(BF16)
    return w


def _rope_tables(pos, reps):
    half = MLA_ROPE // 2
    inv = ROPE_BASE ** (-jnp.arange(half, dtype=F32) * 2.0 / MLA_ROPE)
    ang = pos.astype(F32)[:, None] * inv[None, :]
    cos, sin = jnp.cos(ang), jnp.sin(ang)
    zero = jnp.zeros_like(sin)
    n = 128 // MLA_ROPE
    tabs = (jnp.tile(jnp.concatenate([cos, cos], 1), (reps, n)),
            jnp.tile(jnp.concatenate([zero, sin], 1), (reps, n)),
            jnp.tile(jnp.concatenate([-sin, zero], 1), (reps, n)))
    return tabs


def _layer(x2, b, t, w, tables, mem_k, mem_v, s5_h0, ml_state, conv_buf, paged, final_g, cfg):
    m = b * t
    qw = ML_HEADS * ML_HD
    z = _linear([x2], [w["w_in"]], gamma=w["norm_mix_g"], tm=cfg["tm"])
    z3 = z.reshape(b, t, N_IN_PAD)

    u_t = jnp.transpose(z3[:, :, COL_U:COL_U + 256], (1, 0, 2))
    y_s5_t, s_re, s_im = _s5(u_t, s5_h0[0], s5_h0[1], w, cfg["s5_tc"])
    y_s5 = jnp.transpose(y_s5_t, (1, 0, 2)).reshape(m, 256)

    gates_t = jnp.transpose(z3[:, :, COL_MISC + MISC_IG:COL_MISC + MISC_IG + 2 * ML_HEADS], (0, 2, 1))
    c0, n0, m0 = ml_state
    c0_aug = jnp.concatenate([c0, n0[..., None], jnp.zeros(c0.shape[:3] + (ML_HD - 1,), F32)], axis=-1)
    m0_b = jnp.zeros((b, 8, 128), F32).at[:, :ML_HEADS, :].set(jnp.broadcast_to(m0[:, :, None], (b, ML_HEADS, 128)))
    conv0 = jnp.zeros((b, 8, 2 * qw), F32).at[:, 8 - (ML_CONV - 1):, :].set(conv_buf)
    y_ml, c_aug, m_out = _mlstm(z3, gates_t, conv0, c0_aug, m0_b, w, cfg["ml_chunk"])
    y_ml = y_ml.reshape(m, qw)
    conv_new = jnp.concatenate([conv_buf, z3[:, :, COL_Q:COL_Q + 2 * qw]], axis=1)[:, t:, :]
    ml_c, ml_n, ml_m = c_aug[..., :ML_HD], c_aug[..., ML_HD], m_out[:, :ML_HEADS, 0]

    if paged is None:
        q4, c_lat, kcat, k_rope = _mla_prep(z, tables, b, t, w, cfg["tm"], True)
        kvl = c_lat.shape[1]
        y_mla = _mla_prompt(q4, kcat.reshape(b, t, QK_PAD), w, cfg["tq"])
    else:
        ql, qr, c_lat, kcat, k_rope = _mla_prep(z, tables, b, t, w, cfg["tm"], False)
        kvl = c_lat.shape[1]
        q4 = jnp.concatenate([ql.reshape(b, t, MLA_HEADS, kvl), qr.reshape(b, t, MLA_HEADS, MLA_ROPE),
                              jnp.zeros((b, t, MLA_HEADS, QK_PAD - kvl - MLA_ROPE), BF16)], axis=-1)
        q4 = jnp.transpose(q4, (0, 2, 1, 3))
        cache_c, cache_r, layer, page_table = paged
        y_mla = _mla_sample(q4.reshape(b, MLA_HEADS * t, QK_PAD).astype(F32), kcat.reshape(b, t, QK_PAD), cache_c,
                            cache_r, layer, page_table, w, cfg["pages"])
    y_mla = y_mla.reshape(m, MLA_HEADS * MLA_V)

    x2 = _linear([y_s5, y_ml, y_mla], w["w_out"], residual=x2, tm=cfg["tm"])

    qc = _linear([x2], [w["ca_wq"]], gamma=w["norm_ca_g"], tm=cfg["tm"])
    oc = _cross(qc.reshape(b, t, -1), mem_k, mem_v, cfg["ca_tt"], cfg["kv_t"], cfg["mem_layer"])
    x2 = _linear([oc.reshape(m, -1)], [w["ca_wo"]], residual=x2, tm=cfg["tm"])

    x2 = _mlp(x2, w["norm_ffn_g"], w["ffn_w1"], w["ffn_w2"], w["norm_ffn_g"] if final_g is None else final_g,
              final_g is not None, tm=cfg["tm"])
    pn = S5_STATE
    states = (s_re.reshape(b, -1, pn), s_im.reshape(b, -1, pn), ml_c, ml_n, ml_m, conv_new,
              c_lat.reshape(b, t, kvl), k_rope.reshape(b, t, MLA_ROPE))
    return x2, states


def kernel(x_prompt, x_sample, state_ssm_re, state_ssm_im, state_mlstm_C, state_mlstm_n, state_mlstm_m,
           state_mlstm_conv, cache_kv_latent, cache_k_rope, cache_mem_k, cache_mem_v, page_table, mem_prompt,
           norm_mix_g, w_in, s5_A_re, s5_A_im, s5_log_dt, s5_B_re, s5_B_im, s5_C_re, s5_C_im, s5_D, s5_w_glu,
           s5_out_g, ml_conv_w, ml_conv_b, ml_b_i, ml_b_f, ml_norm_g, mla_q_norm_g, mla_w_uq, mla_kv_norm_g,
           mla_w_uk, mla_w_uv, mla_out_g, w_out, norm_ca_g, ca_mem_g, ca_w_q, ca_w_k, ca_w_v, ca_w_o,
           norm_ffn_g, ffn_w1, ffn_w2, final_norm_g):
    stacked = dict(norm_mix_g=norm_mix_g, w_in=w_in, s5_A_re=s5_A_re, s5_A_im=s5_A_im, s5_log_dt=s5_log_dt,
                   s5_B_re=s5_B_re, s5_B_im=s5_B_im, s5_C_re=s5_C_re, s5_C_im=s5_C_im, s5_D=s5_D,
                   s5_w_glu=s5_w_glu, s5_out_g=s5_out_g, ml_conv_w=ml_conv_w, ml_conv_b=ml_conv_b, ml_b_i=ml_b_i,
                   ml_b_f=ml_b_f, ml_norm_g=ml_norm_g, mla_q_norm_g=mla_q_norm_g, mla_w_uq=mla_w_uq,
                   mla_kv_norm_g=mla_kv_norm_g, mla_w_uk=mla_w_uk, mla_w_uv=mla_w_uv, mla_out_g=mla_out_g,
                   w_out=w_out, norm_ca_g=norm_ca_g, ca_mem_g=ca_mem_g, ca_w_q=ca_w_q, ca_w_k=ca_w_k,
                   ca_w_v=ca_w_v, ca_w_o=ca_w_o, norm_ffn_g=norm_ffn_g, ffn_w1=ffn_w1, ffn_w2=ffn_w2)
    depth = w_in.shape[0]
    layers = [_prep_layer({k: v[l] for k, v in stacked.items()}) for l in range(depth)]

    bp, tp, d = x_prompt.shape
    bs, ts, _ = x_sample.shape
    gp = s5_A_re.shape[1] * s5_A_re.shape[2]
    qw = ML_HEADS * ML_HD
    n_mem = mem_prompt.shape[1]
    past_len = page_table.shape[1] * cache_kv_latent.shape[2]

    cfg_p = dict(tm=512, s5_tc=64, ml_chunk=128, tq=256, ca_tt=512, pages=1, kv_t=False, mem_layer=None)
    tab_p = _rope_tables(jnp.arange(tp, dtype=jnp.int32), 1)
    zero_s5 = (jnp.zeros((bp, gp), F32), jnp.zeros((bp, gp), F32))
    zero_ml = (jnp.zeros((bp, ML_HEADS, ML_HD, ML_HD), F32), jnp.zeros((bp, ML_HEADS, ML_HD), F32),
               jnp.zeros((bp, ML_HEADS), F32))
    zero_conv = jnp.zeros((bp, ML_CONV - 1, 2 * qw), F32)
    xp = x_prompt.reshape(bp * tp, d)
    mem2 = mem_prompt.reshape(bp * n_mem, d)
    p_states = []
    for l, w in enumerate(layers):
        mkv = _linear([mem2], [w["ca_wkv"]], gamma=w["ca_mem_g"], tm=512)
        wd = mkv.shape[1] // 2
        mk = mkv[:, :wd].reshape(bp, n_mem, wd)
        mv = mkv[:, wd:].reshape(bp, n_mem, wd)
        xp, st = _layer(xp, bp, tp, w, tab_p, mk, mv, zero_s5, zero_ml, zero_conv, None,
                        final_norm_g if l == depth - 1 else None, cfg_p)
        p_states.append(st + (mk.reshape(bp, n_mem, CA_HEADS, CA_HD), mv.reshape(bp, n_mem, CA_HEADS, CA_HD)))
    y_prompt = xp.reshape(bp, tp, d)
    p_out = [jnp.stack([s[i] for s in p_states]) for i in range(10)]

    cfg_s = dict(tm=512, s5_tc=ts, ml_chunk=ts, tq=ts, ca_tt=ts, pages=32, kv_t=True)
    tab_s = _rope_tables(past_len + jnp.arange(ts, dtype=jnp.int32), bs)
    xs = x_sample.reshape(bs * ts, d)
    cache_rope_t = jnp.swapaxes(cache_k_rope, 2, 3)
    mem_k_t = jnp.transpose(cache_mem_k, (0, 1, 3, 4, 2)).reshape(depth, bs, -1, n_mem)
    mem_v_t = jnp.transpose(cache_mem_v, (0, 1, 3, 4, 2)).reshape(depth, bs, -1, n_mem)
    s_states = []
    for l, w in enumerate(layers):
        cfg_s["mem_layer"] = l
        xs, st = _layer(xs, bs, ts, w, tab_s, mem_k_t, mem_v_t,
                        (state_ssm_re[l].reshape(bs, gp), state_ssm_im[l].reshape(bs, gp)),
                        (state_mlstm_C[l], state_mlstm_n[l], state_mlstm_m[l]), state_mlstm_conv[l],
                        (cache_kv_latent, cache_rope_t, l, page_table),
                        final_norm_g if l == depth - 1 else None, cfg_s)
        s_states.append(st)
    y_sample = xs.reshape(bs, ts, d)
    s_out = [jnp.stack([s[i] for s in s_states]) for i in range(8)]

    return (y_prompt, y_sample, *p_out, *s_out)
```

```python
import functools
import math

import jax
import jax.numpy as jnp
import numpy as np
from jax import lax
from jax.experimental import pallas as pl
from jax.experimental.pallas import tpu as pltpu

F32 = jnp.float32
BF16 = jnp.bfloat16
NORM_EPS = 1e-6
ROPE_BASE = 10000.0

S5_CH = 16
S5_STATE = 64
ML_HEADS = 4
ML_HD = 64
ML_CONV = 4
MLA_HEADS = 8
MLA_NOPE = 64
MLA_ROPE = 32
MLA_V = 64
CA_HEADS = 4
CA_HD = 64
QK_PAD = 256

COL_U, COL_Q, COL_K, COL_V, COL_O, COL_CQ, COL_CKV, COL_MISC = 0, 256, 512, 768, 1024, 1280, 1536, 1664
N_IN_PAD = 1792
MISC_KR, MISC_IG, MISC_FG = 0, 32, 36

VMEM_LIMIT = 56 * 1024 * 1024


def _cparams(*sem):
    return pltpu.CompilerParams(dimension_semantics=sem, vmem_limit_bytes=VMEM_LIMIT)


def _rms(x, g):
    return x * lax.rsqrt(jnp.mean(x * x, axis=-1, keepdims=True) + NORM_EPS) * g


def _dot(a, b):
    return jnp.dot(a, b, preferred_element_type=F32)


def _dot_nt(a, b):
    return lax.dot_general(a, b, (((1,), (1,)), ((), ())), preferred_element_type=F32)


def _dot_tn(a, b):
    return lax.dot_general(a, b, (((0,), (0,)), ((), ())), preferred_element_type=F32)


def _const_spec(shape):
    nd = len(shape)
    return pl.BlockSpec(shape, lambda *_: (0,) * nd)


def _linear_body(*refs, n_in, has_norm, has_res, has_post):
    x_refs = refs[:n_in]
    pos = n_in
    g_ref = refs[pos] if has_norm else None
    pos += int(has_norm)
    w_refs = refs[pos:pos + n_in]
    pos += n_in
    res_ref = refs[pos] if has_res else None
    pos += int(has_res)
    post_refs = refs[pos:pos + 2] if has_post else None
    pos += 2 * int(has_post)
    o_ref = refs[pos]
    acc = None
    for x_ref, w_ref in zip(x_refs, w_refs):
        x = x_ref[...]
        if has_norm:
            x = _rms(x, g_ref[...])
        p = _dot(x.astype(BF16), w_ref[...])
        acc = p if acc is None else acc + p
    if has_res:
        acc = acc + res_ref[...]
    o_ref[...] = acc
    if has_post:
        refs[pos + 1][...] = _dot(_rms(acc, post_refs[0][...]).astype(BF16), post_refs[1][...])


def _linear(xs, ws, gamma=None, residual=None, post=None, tm=512):
    m = xs[0].shape[0]
    n = ws[0].shape[1]
    tm = min(tm, m)
    assert m % tm == 0
    in_specs = [pl.BlockSpec((tm, x.shape[1]), lambda i: (i, 0)) for x in xs]
    args = list(xs)
    if gamma is not None:
        in_specs.append(_const_spec((1, xs[0].shape[1])))
        args.append(gamma.reshape(1, -1))
    for w in ws:
        in_specs.append(_const_spec(w.shape))
        args.append(w)
    if residual is not None:
        in_specs.append(pl.BlockSpec((tm, n), lambda i: (i, 0)))
        args.append(residual)
    out_specs = [pl.BlockSpec((tm, n), lambda i: (i, 0))]
    out_shape = [jax.ShapeDtypeStruct((m, n), F32)]
    if post is not None:
        in_specs += [_const_spec((1, n)), _const_spec(post[1].shape)]
        args += [post[0].reshape(1, -1), post[1]]
        out_specs.append(pl.BlockSpec((tm, post[1].shape[1]), lambda i: (i, 0)))
        out_shape.append(jax.ShapeDtypeStruct((m, post[1].shape[1]), F32))
    body = functools.partial(_linear_body, n_in=len(xs), has_norm=gamma is not None, has_res=residual is not None,
                             has_post=post is not None)
    outs = pl.pallas_call(
        body,
        grid=(m // tm,),
        in_specs=in_specs,
        out_specs=out_specs,
        out_shape=out_shape,
        compiler_params=_cparams("parallel"),
        name="linear",
    )(*args)
    return outs if post is not None else outs[0]


def _mlp_body(x_ref, a_ref, wa_ref, g_ref, w1_ref, w2_ref, gf_ref, o_ref, *, fc, final_norm):
    x = x_ref[...] + _dot(a_ref[...].astype(BF16), wa_ref[...])
    h = _rms(x, g_ref[...]).astype(BF16)
    acc = x
    for c in range(w1_ref.shape[1] // fc):
        a = _dot(h, w1_ref[:, c * fc:(c + 1) * fc])
        a = jnp.maximum(a, 0.0)
        acc = acc + _dot((a * a).astype(BF16), w2_ref[c * fc:(c + 1) * fc, :])
    if final_norm:
        acc = _rms(acc, gf_ref[...])
    o_ref[...] = acc


def _mlp(x, a, wa, gamma, w1, w2, final_gamma, final_norm, tm=512, fc=1024):
    m, d = x.shape
    dff = w1.shape[1]
    tm = min(tm, m)
    fc = min(fc, dff)
    body = functools.partial(_mlp_body, fc=fc, final_norm=final_norm)
    return pl.pallas_call(
        body,
        grid=(m // tm,),
        in_specs=[
            pl.BlockSpec((tm, d), lambda i: (i, 0)),
            pl.BlockSpec((tm, a.shape[1]), lambda i: (i, 0)),
            _const_spec(wa.shape),
            _const_spec((1, d)),
            pl.BlockSpec((d, dff), lambda i: (0, 0), pipeline_mode=pl.Buffered(1)),
            pl.BlockSpec((dff, d), lambda i: (0, 0), pipeline_mode=pl.Buffered(1)),
            _const_spec((1, d)),
        ],
        out_specs=pl.BlockSpec((tm, d), lambda i: (i, 0)),
        out_shape=jax.ShapeDtypeStruct((m, d), F32),
        compiler_params=_cparams("parallel"),
        name="mlp",
    )(x, a, wa, gamma.reshape(1, -1), w1, w2, final_gamma.reshape(1, -1))


def _s5_body(u_ref, h0r_ref, h0i_ref, are_ref, aim_ref, ldt_ref, br_ref, bi_ref, cr_ref, ci_ref, d_ref, wg_ref,
             go_ref, y_ref, hro_ref, hio_ref, xr_s, xi_s, hr_s, hi_s, *, tc, bb):
    @pl.when(pl.program_id(1) == 0)
    def _():
        hr_s[...] = h0r_ref[...]
        hi_s[...] = h0i_ref[...]

    ar = are_ref[...]
    ai = aim_ref[...]
    dt = jnp.exp(ldt_ref[...])
    mag = jnp.exp(ar * dt)
    lr = mag * jnp.cos(ai * dt)
    li = mag * jnp.sin(ai * dt)
    den = ar * ar + ai * ai
    zr = lr - 1.0
    fr = (zr * ar + li * ai) / den
    fi = (li * ar - zr * ai) / den

    ch = u_ref.shape[2]
    u = u_ref[...].reshape(tc * bb, ch)
    ub = u.astype(BF16)
    pr = _dot(ub, br_ref[...])
    pi = _dot(ub, bi_ref[...])
    xr_s[...] = fr * pr - fi * pi
    xi_s[...] = fr * pi + fi * pr

    def step(t, carry):
        hr, hi = carry
        rows = pl.ds(pl.multiple_of(t * bb, bb), bb)
        nr = lr * hr - li * hi + xr_s[rows, :]
        ni = lr * hi + li * hr + xi_s[rows, :]
        xr_s[rows, :] = nr
        xi_s[rows, :] = ni
        return nr, ni

    hr, hi = lax.fori_loop(0, tc, step, (hr_s[...], hi_s[...]))
    hr_s[...] = hr
    hi_s[...] = hi
    hro_ref[...] = hr
    hio_ref[...] = hi

    y = _dot(xr_s[...].astype(BF16), cr_ref[...]) - _dot(xi_s[...].astype(BF16), ci_ref[...])
    y = jax.nn.gelu(y + d_ref[...] * u)
    g = _dot(y.astype(BF16), wg_ref[...])
    o = g[:, :ch] * jax.nn.sigmoid(g[:, ch:])
    y_ref[...] = _rms(o, go_ref[...]).reshape(tc, bb, ch)


def _s5(u_t, h0r, h0i, w, tc):
    t, b, ch = u_t.shape
    gp = h0r.shape[1]
    bb = b if b <= 128 else 128
    tc = min(tc, t)
    body = functools.partial(_s5_body, tc=tc, bb=bb)
    row = lambda n: _const_spec((1, n))
    return pl.pallas_call(
        body,
        grid=(b // bb, t // tc),
        in_specs=[
            pl.BlockSpec((tc, bb, ch), lambda i, j: (j, i, 0)),
            pl.BlockSpec((bb, gp), lambda i, j: (i, 0)),
            pl.BlockSpec((bb, gp), lambda i, j: (i, 0)),
            row(gp), row(gp), row(gp),
            _const_spec((ch, gp)), _const_spec((ch, gp)),
            _const_spec((gp, ch)), _const_spec((gp, ch)),
            row(ch), _const_spec((ch, 2 * ch)), row(ch),
        ],
        out_specs=[
            pl.BlockSpec((tc, bb, ch), lambda i, j: (j, i, 0)),
            pl.BlockSpec((bb, gp), lambda i, j: (i, 0)),
            pl.BlockSpec((bb, gp), lambda i, j: (i, 0)),
        ],
        out_shape=[
            jax.ShapeDtypeStruct((t, b, ch), F32),
            jax.ShapeDtypeStruct((b, gp), F32),
            jax.ShapeDtypeStruct((b, gp), F32),
        ],
        scratch_shapes=[
            pltpu.VMEM((tc * bb, gp), F32), pltpu.VMEM((tc * bb, gp), F32),
            pltpu.VMEM((bb, gp), F32), pltpu.VMEM((bb, gp), F32),
        ],
        compiler_params=_cparams("parallel", "arbitrary"),
        name="s5",
    )(u_t, h0r, h0i, w["s5_are"], w["s5_aim"], w["s5_ldt"], w["s5_br"], w["s5_bi"], w["s5_cr"], w["s5_ci"],
      w["s5_d"], w["s5_wglu"], w["s5_go"])


def _split3(x):
    hi = x.astype(BF16)
    r1 = x - hi.astype(F32)
    mid = r1.astype(BF16)
    lo = (r1 - mid.astype(F32)).astype(BF16)
    return hi, mid, lo


def _mlstm_body(zq_ref, zk_ref, zv_ref, zo_ref, misc_ref, gt_ref, conv0_ref, c0_ref, m0_ref, cw_ref, cb_ref,
                bcol_ref, brow_ref, gn_ref, psel_ref, y_ref, co_ref, mo_ref, padq, padk, cs, ms, *, cl):
    hd = ML_HD
    qw = ML_HEADS * hd

    @pl.when(pl.program_id(1) == 0)
    def _():
        padq[0:8, :] = conv0_ref[0][:, :qw]
        padk[0:8, :] = conv0_ref[0][:, qw:]
        cs[...] = c0_ref[0]
        ms[...] = m0_ref[0]

    padq[8:8 + cl, :] = zq_ref[0]
    padk[8:8 + cl, :] = zk_ref[0]
    cw = cw_ref[...]
    cb = cb_ref[...]

    def conv(pad, w, b):
        y = b
        for j in range(ML_CONV):
            y = y + pad[8 - (ML_CONV - 1) + j:8 - (ML_CONV - 1) + j + cl, :] * w[j:j + 1, :]
        return y

    q = jax.nn.silu(conv(padq, cw[:, :qw], cb[:, :qw]))
    k = jax.nn.silu(conv(padk, cw[:, qw:], cb[:, qw:])) * (hd ** -0.5)
    if cl >= 8:
        padq[0:8, :] = padq[cl:cl + 8, :]
        padk[0:8, :] = padk[cl:cl + 8, :]

    pw = 2 * hd
    gcol = misc_ref[0] + bcol_ref[...]
    grow = gt_ref[0] + brow_ref[:, 0:1]
    ig_row = grow[0:ML_HEADS]
    lf_row = jax.nn.log_sigmoid(grow[ML_HEADS:2 * ML_HEADS])

    ri = lax.broadcasted_iota(jnp.int32, (cl, cl), 0)
    ci = lax.broadcasted_iota(jnp.int32, (cl, cl), 1)
    tril = ri >= ci
    lower = jnp.where(tril, 1.0, 0.0).astype(BF16)
    upper = jnp.where(ri <= ci, 1.0, 0.0).astype(BF16)
    bc_col = sum(_dot(lower, part) for part in _split3(jax.nn.log_sigmoid(gcol)))
    bc_row = sum(_dot(part, upper) for part in _split3(lf_row))
    key_row = ig_row - bc_row
    lane = lax.broadcasted_iota(jnp.int32, (cl, pw), 1)
    gate_src = jnp.where(lane < MISC_FG, gcol, bc_col)
    rep = sum(_dot(part, psel_ref[...]) for part in _split3(gate_src))

    first_half = lane < hd
    blk = (lax.broadcasted_iota(jnp.int32, (pw, pw), 0) // hd) == (lax.broadcasted_iota(jnp.int32, (pw, pw), 1) // hd)
    blk2 = jnp.concatenate([blk, blk], axis=1)
    mean_blk = jnp.where(blk, 1.0 / hd, 0.0).astype(BF16)
    ones_slab = jnp.ones((cl, pw), BF16)
    gn = gn_ref[...]
    for j in range(ML_HEADS // 2):
        ps = slice(j * pw, (j + 1) * pw)
        q2 = q[:, ps]
        k2 = k[:, ps]
        k2b = k2.astype(BF16)
        vo2 = jnp.concatenate([zv_ref[0][:, ps].astype(BF16), ones_slab], axis=1)
        c_pair = cs[j]
        qc = _dot(q2.astype(BF16), c_pair.astype(BF16))
        hh2 = w_s2 = decay2 = None
        for a in range(2):
            h = 2 * j + a
            ig_rep = rep[:, h * pw:(h + 1) * pw]
            bc_rep = rep[:, (ML_HEADS + h) * pw:(ML_HEADS + h + 1) * pw]
            m_prev = ms[h:h + 1, :]
            d = jnp.where(tril, bc_rep[:, :cl] + key_row[h:h + 1, :], -jnp.inf)
            inter = bc_rep + m_prev
            m_tok = jnp.maximum(inter, jnp.max(d, axis=1, keepdims=True))
            w_intra = jnp.exp(d - m_tok[:, :cl])
            w_inter = jnp.exp(inter - m_tok)
            mine = first_half if a == 0 else jnp.logical_not(first_half)
            s = _dot_nt(jnp.where(mine, q2, 0.0).astype(BF16), k2b) * w_intra
            nd = _dot(s.astype(BF16), vo2)
            num = nd[:, :pw] + w_inter * qc[:, :pw]
            den = nd[:, pw:] + w_inter * qc[:, pw:]
            hh = num / jnp.maximum(jnp.abs(den), jnp.exp(-m_tok))
            m_end = m_tok[cl - 1:cl, :]
            g_end = bc_rep[cl - 1:cl, :]
            w_s = jnp.exp(g_end - bc_rep + ig_rep - m_end)
            decay = jnp.exp(g_end + m_prev - m_end)
            ms[h:h + 1, :] = m_end
            if a == 0:
                hh2, w_s2, decay2 = hh, w_s, decay
            else:
                hh2 = jnp.where(first_half, hh2, hh)
                w_s2 = jnp.where(first_half, w_s2, w_s)
                decay2 = jnp.where(first_half[0:1, :], decay2, decay)
        upd = _dot_tn((k2 * w_s2).astype(BF16), vo2)
        cs[j] = jnp.concatenate([decay2, decay2], axis=1) * c_pair + jnp.where(blk2, upd, 0.0)
        oh = jax.nn.sigmoid(zo_ref[0][:, ps]) * hh2
        sq = oh * oh
        sq_hi = sq.astype(BF16)
        msq = _dot(sq_hi, mean_blk) + _dot((sq - sq_hi.astype(F32)).astype(BF16), mean_blk)
        y_ref[0, :, ps] = oh * lax.rsqrt(msq + NORM_EPS) * gn[:, ps]
    co_ref[0] = cs[...]
    mo_ref[0] = ms[...]


def _mlstm_state_in(c, n):
    b = c.shape[0]
    eye2 = jnp.eye(2, dtype=F32)
    c5 = c.reshape(b, ML_HEADS // 2, 2, ML_HD, ML_HD)
    n5 = jnp.broadcast_to(n.reshape(b, ML_HEADS // 2, 2, ML_HD, 1), c5.shape)
    blockdiag = lambda x: jnp.einsum("bjadv,ac->bjadcv", x, eye2).reshape(b, ML_HEADS // 2, 2 * ML_HD, 2 * ML_HD)
    return jnp.concatenate([blockdiag(c5), blockdiag(n5)], axis=-1)


def _mlstm_state_out(cp):
    b = cp.shape[0]
    eye2 = jnp.eye(2, dtype=F32)
    x = cp.reshape(b, ML_HEADS // 2, 2, ML_HD, 2, 2, ML_HD)
    diag = jnp.einsum("bjadkcv,ac->bjadkv", x, eye2)
    c = diag[:, :, :, :, 0, :].reshape(b, ML_HEADS, ML_HD, ML_HD)
    n = diag[:, :, :, :, 1, 0].reshape(b, ML_HEADS, ML_HD)
    return c, n


def _mlstm(z3, gates_t, conv0, c0_pair, m0, w, cl):
    b, t, _ = z3.shape
    qw = ML_HEADS * ML_HD
    cl = min(cl, t)
    assert cl <= 2 * ML_HD
    st_block = (1, ML_HEADS // 2, 2 * ML_HD, 4 * ML_HD)
    body = functools.partial(_mlstm_body, cl=cl)
    zspec = lambda col: pl.BlockSpec((1, cl, qw), lambda i, j: (i, j, col // qw))
    return pl.pallas_call(
        body,
        grid=(b, t // cl),
        in_specs=[
            zspec(COL_Q), zspec(COL_K), zspec(COL_V), zspec(COL_O),
            pl.BlockSpec((1, cl, 128), lambda i, j: (i, j, COL_MISC // 128)),
            pl.BlockSpec((1, 8, cl), lambda i, j: (i, 0, j)),
            pl.BlockSpec((1, 8, 2 * qw), lambda i, j: (i, 0, 0)),
            pl.BlockSpec(st_block, lambda i, j: (i, 0, 0, 0)),
            pl.BlockSpec((1, 8, 128), lambda i, j: (i, 0, 0)),
            _const_spec((ML_CONV, 2 * qw)), _const_spec((1, 2 * qw)),
            _const_spec((1, 128)), _const_spec((8, 128)), _const_spec((1, qw)), _const_spec(w["ml_psel"].shape),
        ],
        out_specs=[
            pl.BlockSpec((1, cl, qw), lambda i, j: (i, j, 0)),
            pl.BlockSpec(st_block, lambda i, j: (i, 0, 0, 0)),
            pl.BlockSpec((1, 8, 128), lambda i, j: (i, 0, 0)),
        ],
        out_shape=[
            jax.ShapeDtypeStruct((b, t, qw), F32),
            jax.ShapeDtypeStruct((b,) + st_block[1:], F32),
            jax.ShapeDtypeStruct((b, 8, 128), F32),
        ],
        scratch_shapes=[
            pltpu.VMEM((cl + 8, qw), F32), pltpu.VMEM((cl + 8, qw), F32),
            pltpu.VMEM(st_block[1:], F32), pltpu.VMEM((8, 128), F32),
        ],
        compiler_params=_cparams("parallel", "arbitrary"),
        name="mlstm",
    )(z3, z3, z3, z3, z3, gates_t, conv0, c0_pair, m0, w["ml_cw"], w["ml_cb"], w["ml_bcol"], w["ml_brow"], w["ml_gn"],
      w["ml_psel"])


def _rope128(x, cos_t, sin_up, sin_dn):
    half = MLA_ROPE // 2
    return x * cos_t + pltpu.roll(x, half, 1) * sin_up + pltpu.roll(x, 128 - half, 1) * sin_dn


def _mla_prep_body(zcq_ref, zckv_ref, misc_ref, cos_ref, sup_ref, sdn_ref, gq_ref, wn_ref, wr_ref, wuk_ref, gkv_ref,
                   sel_ref, *out_refs, heads_major):
    cos_t, sup, sdn = cos_ref[...], sup_ref[...], sdn_ref[...]
    cq = _rms(zcq_ref[...], gq_ref[...]).astype(BF16)
    qn = _dot(cq, wn_ref[...])
    qrp = _dot(cq, wr_ref[...])
    ql = [_dot(qn[:, 128 * j:128 * (j + 1)].astype(BF16), wuk_ref[j]).astype(BF16) for j in range(wuk_ref.shape[0])]
    qr = [_rope128(qrp[:, 128 * j:128 * (j + 1)], cos_t, sup, sdn).astype(BF16) for j in range(qrp.shape[1] // 128)]
    if heads_major:
        q_ref, c_ref, kcat_ref, kr_ref = out_refs
        kvl = ql[0].shape[1] // 2
        qr_all = jnp.concatenate(qr, axis=1)
        for h in range(MLA_HEADS):
            q_ref[0, h, :, 0:kvl] = ql[h // 2][:, (h % 2) * kvl:(h % 2 + 1) * kvl]
            q_ref[0, h, :, kvl:QK_PAD] = _dot(qr_all, sel_ref[h]).astype(BF16)
    else:
        ql_ref, qr_ref, c_ref, kcat_ref, kr_ref = out_refs
        for j, v in enumerate(ql):
            ql_ref[:, 256 * j:256 * (j + 1)] = v
        for j, v in enumerate(qr):
            qr_ref[:, 128 * j:128 * (j + 1)] = v
    c = _rms(zckv_ref[...], gkv_ref[...])
    c_ref[...] = c
    krf = _rope128(misc_ref[...], cos_t, sup, sdn)
    kr_ref[...] = krf[:, MISC_KR:MISC_KR + MLA_ROPE]
    lane = lax.broadcasted_iota(jnp.int32, krf.shape, 1)
    kcat_ref[:, 0:128] = c.astype(BF16)
    kcat_ref[:, 128:256] = jnp.where(lane < MLA_ROPE, krf, 0.0).astype(BF16)


def _mla_prep(z, tables, b, t_len, w, tm, heads_major):
    m = z.shape[0]
    tm = min(tm, m, t_len) if heads_major else m
    nt = tables[0].shape[0] // tm
    tspec = pl.BlockSpec((tm, 128), lambda i: (i % nt, 0))
    kvl = 128
    if heads_major:
        q_specs = [pl.BlockSpec((1, MLA_HEADS, tm, QK_PAD), lambda i: (i // nt, 0, i % nt, 0))]
        q_shapes = [jax.ShapeDtypeStruct((b, MLA_HEADS, t_len, QK_PAD), BF16)]
    else:
        q_specs = [pl.BlockSpec((tm, MLA_HEADS * kvl), lambda i: (i, 0)),
                   pl.BlockSpec((tm, MLA_HEADS * MLA_ROPE), lambda i: (i, 0))]
        q_shapes = [jax.ShapeDtypeStruct((m, MLA_HEADS * kvl), BF16),
                    jax.ShapeDtypeStruct((m, MLA_HEADS * MLA_ROPE), BF16)]
    return pl.pallas_call(
        functools.partial(_mla_prep_body, heads_major=heads_major),
        grid=(m // tm,),
        in_specs=[
            pl.BlockSpec((tm, 256), lambda i: (i, COL_CQ // 256)),
            pl.BlockSpec((tm, kvl), lambda i: (i, COL_CKV // 128)),
            pl.BlockSpec((tm, 128), lambda i: (i, COL_MISC // 128)),
            tspec, tspec, tspec,
            _const_spec((1, 256)), _const_spec(w["mla_wn"].shape), _const_spec(w["mla_wr"].shape),
            _const_spec(w["mla_wuk"].shape), _const_spec((1, kvl)), _const_spec(w["mla_sel"].shape),
        ],
        out_specs=q_specs + [
            pl.BlockSpec((tm, kvl), lambda i: (i, 0)),
            pl.BlockSpec((tm, QK_PAD), lambda i: (i, 0)),
            pl.BlockSpec((tm, MLA_ROPE), lambda i: (i, 0)),
        ],
        out_shape=q_shapes + [
            jax.ShapeDtypeStruct((m, kvl), F32),
            jax.ShapeDtypeStruct((m, QK_PAD), BF16),
            jax.ShapeDtypeStruct((m, MLA_ROPE), F32),
        ],
        compiler_params=_cparams("parallel"),
        name="mla_prep",
    )(z, z, z, tables[0], tables[1], tables[2], w["mla_gq"], w["mla_wn"], w["mla_wr"], w["mla_wuk"], w["mla_gkv"],
      w["mla_sel"])


def _mla_out(o_heads, wuv_ref, g_ref):
    ys = []
    for j in range(MLA_HEADS // 2):
        pair = jnp.concatenate([o_heads[2 * j], o_heads[2 * j + 1]], axis=1).astype(BF16)
        ys.append(_dot(pair, wuv_ref[j]))
    return _rms(jnp.concatenate(ys, axis=1), g_ref[...])


def _mla_prompt_body(q_ref, k_ref, wuv_ref, g_ref, o_ref, m_s, l_s, acc_s, *, tq, scale):
    i = pl.program_id(1)
    kvl = acc_s.shape[1]
    scale_log2e = scale * math.log2(math.e)
    m_s[...] = jnp.full(m_s.shape, -jnp.inf, F32)
    l_s[...] = jnp.zeros(l_s.shape, F32)
    acc_s[...] = jnp.zeros(acc_s.shape, F32)

    def block(j, masked):
        kb = k_ref[0, pl.ds(pl.multiple_of(j * tq, tq), tq), :]
        vb = kb[:, :kvl]
        if masked:
            causal = (lax.broadcasted_iota(jnp.int32, (tq, tq), 1) <= lax.broadcasted_iota(jnp.int32, (tq, tq), 0))
        for h in range(MLA_HEADS):
            rows = slice(h * tq, (h + 1) * tq)
            s = _dot_nt(q_ref[0, h], kb)
            if masked:
                s = jnp.where(causal, s, -jnp.inf)
            m_prev = m_s[rows, :]
            m_next = jnp.maximum(m_prev, jnp.max(s, axis=1, keepdims=True))
            alpha = jnp.exp2((m_prev - m_next) * scale_log2e)
            p = jnp.exp2((s - jnp.concatenate([m_next] * (tq // kvl), axis=1)) * scale_log2e)
            p_lanes = p[:, 0:kvl]
            for c in range(1, tq // kvl):
                p_lanes = p_lanes + p[:, c * kvl:(c + 1) * kvl]
            l_s[rows, :] = alpha * l_s[rows, :] + p_lanes
            acc_s[rows, :] = alpha * acc_s[rows, :] + _dot(p.astype(BF16), vb)
            m_s[rows, :] = m_next

    def full_block(j, carry):
        block(j, False)
        return carry

    lax.fori_loop(0, i, full_block, 0)
    block(i, True)
    o = acc_s[...] / jnp.sum(l_s[...], axis=1, keepdims=True)
    o_ref[0] = _mla_out([o[h * tq:(h + 1) * tq] for h in range(MLA_HEADS)], wuv_ref, g_ref)


def _mla_prompt(q, kcat, w, tq):
    b, hn, t, qk = q.shape
    tq = min(tq, t)
    kvl = 128
    wo = MLA_HEADS * MLA_V
    body = functools.partial(_mla_prompt_body, tq=tq, scale=1.0 / math.sqrt(MLA_NOPE + MLA_ROPE))
    return pl.pallas_call(
        body,
        grid=(b, t // tq),
        in_specs=[
            pl.BlockSpec((1, hn, tq, qk), lambda i, j: (i, 0, j, 0)),
            pl.BlockSpec((1, t, qk), lambda i, j: (i, 0, 0)),
            _const_spec(w["mla_wuv"].shape), _const_spec((1, wo)),
        ],
        out_specs=pl.BlockSpec((1, tq, wo), lambda i, j: (i, j, 0)),
        out_shape=jax.ShapeDtypeStruct((b, t, wo), F32),
        scratch_shapes=[pltpu.VMEM((hn * tq, kvl), F32), pltpu.VMEM((hn * tq, kvl), F32),
                        pltpu.VMEM((hn * tq, kvl), F32)],
        compiler_params=_cparams("parallel", "parallel"),
        name="mla_prompt",
    )(q, kcat, w["mla_wuv"], w["mla_go"])


def _mla_sample_body(pt_ref, q_ref, kn_ref, cc_hbm, cr_hbm, wuv_ref, g_ref, o_ref, cbuf, rbuf, sem, m_s, l_s, acc_s,
                     *, layer, group, n_groups, t_new, scale):
    b = pl.program_id(0)
    kvl = acc_s.shape[1]
    page = cbuf.shape[1] // group

    def group_copies(bi, g, slot):
        cps = []
        for i in range(group):
            pid = pt_ref[bi, g * group + i]
            cps.append(pltpu.make_async_copy(cc_hbm.at[layer, pid], cbuf.at[slot, pl.ds(i * page, page), :],
                                             sem.at[slot]))
            cps.append(pltpu.make_async_copy(cr_hbm.at[layer, pid], rbuf.at[slot, :, pl.ds(i * page, page)],
                                             sem.at[slot]))
        return cps

    def start_group(bi, g, slot):
        for cp in group_copies(bi, g, slot):
            cp.start()

    def wait_group(bi, g, slot):
        for cp in group_copies(bi, g, slot):
            cp.wait()

    @pl.when(b == 0)
    def _():
        start_group(0, 0, 0)

    m_s[...] = jnp.full(m_s.shape, -jnp.inf, F32)
    l_s[...] = jnp.zeros(l_s.shape, F32)
    acc_s[...] = jnp.zeros(acc_s.shape, F32)
    q = q_ref[0].astype(BF16)
    q_lat = q[:, 0:kvl]
    q_rope = q[:, kvl:kvl + MLA_ROPE]

    first = lax.rem(b * n_groups, 2)
    for g in range(n_groups):
        slot = lax.rem(first + g, 2)
        if g + 1 < n_groups:
            start_group(b, g + 1, 1 - slot)
        else:
            @pl.when(b + 1 < pl.num_programs(0))
            def _():
                start_group(b + 1, 0, 1 - slot)
        wait_group(b, g, slot)
        cb = cbuf[slot].astype(BF16)
        rb = rbuf[slot].astype(BF16)
        s = (_dot_nt(q_lat, cb) + _dot(q_rope, rb)) * scale
        m_old = m_s[...]
        m_new = jnp.maximum(m_old, jnp.max(s, axis=1, keepdims=True))
        alpha = jnp.exp(m_old - m_new)
        p = jnp.exp(s - m_new)
        l_s[...] = alpha * l_s[...] + jnp.sum(p, axis=1, keepdims=True)
        acc_s[...] = alpha * acc_s[...] + _dot(p.astype(BF16), cb)
        m_s[...] = m_new

    kn = kn_ref[0].astype(F32)
    qf = q.astype(F32)
    t_row = lax.broadcasted_iota(jnp.int32, (q.shape[0], 1), 0) % t_new
    s_new = []
    for t2 in range(t_new):
        st = jnp.sum(qf * kn[t2:t2 + 1, :], axis=1, keepdims=True) * scale
        s_new.append(jnp.where(t_row >= t2, st, -jnp.inf))
    m_o = m_s[...]
    m_n = m_o
    for st in s_new:
        m_n = jnp.maximum(m_n, st)
    al = jnp.exp(m_o - m_n)
    l = al * l_s[...]
    acc = al * acc_s[...]
    for t2, st in enumerate(s_new):
        pt = jnp.exp(st - m_n)
        l = l + pt
        acc = acc + pt * kn[t2:t2 + 1, 0:kvl]
    acc_s[...] = acc / l
    o_ref[0] = _mla_out([acc_s[h * t_new:(h + 1) * t_new, :] for h in range(MLA_HEADS)], wuv_ref, g_ref)


def _mla_sample(q, knew, cache_c, cache_r, layer, page_table, w, pages):
    b, rows, qk = q.shape
    t_new = knew.shape[1]
    n_pages = page_table.shape[1]
    page, kvl = cache_c.shape[2], cache_c.shape[3]
    group = min(pages, n_pages)
    assert n_pages % group == 0
    wo = MLA_HEADS * MLA_V
    body = functools.partial(_mla_sample_body, layer=layer, group=group, n_groups=n_pages // group, t_new=t_new,
                             scale=1.0 / math.sqrt(MLA_NOPE + MLA_ROPE))
    grid_spec = pltpu.PrefetchScalarGridSpec(
        num_scalar_prefetch=1,
        grid=(b,),
        in_specs=[
            pl.BlockSpec((1, rows, qk), lambda bi, pt: (bi, 0, 0)),
            pl.BlockSpec((1, t_new, qk), lambda bi, pt: (bi, 0, 0)),
            pl.BlockSpec(memory_space=pl.ANY),
            pl.BlockSpec(memory_space=pl.ANY),
            pl.BlockSpec(w["mla_wuv"].shape, lambda bi, pt: (0, 0, 0)),
            pl.BlockSpec((1, wo), lambda bi, pt: (0, 0)),
        ],
        out_specs=pl.BlockSpec((1, t_new, wo), lambda bi, pt: (bi, 0, 0)),
        scratch_shapes=[
            pltpu.VMEM((2, group * page, kvl), F32),
            pltpu.VMEM((2, MLA_ROPE, group * page), F32),
            pltpu.SemaphoreType.DMA((2,)),
            pltpu.VMEM((rows, 1), F32), pltpu.VMEM((rows, 1), F32), pltpu.VMEM((rows, kvl), F32),
        ],
    )
    return pl.pallas_call(
        body,
        grid_spec=grid_spec,
        out_shape=jax.ShapeDtypeStruct((b, t_new, wo), F32),
        compiler_params=_cparams("arbitrary"),
        name="mla_sample",
    )(page_table, q, knew, cache_c, cache_r, w["mla_wuv"], w["mla_go"])


def _cross_body(q_ref, k_ref, v_ref, o_ref, *, kv_t):
    q = q_ref[0]
    kb = k_ref[...].reshape(k_ref.shape[-2:]).astype(BF16)
    vb = v_ref[...].reshape(v_ref.shape[-2:]).astype(BF16)
    lane = lax.broadcasted_iota(jnp.int32, q.shape, 1)
    out = jnp.zeros(q.shape, F32)
    for h in range(CA_HEADS):
        sel = (lane >= h * CA_HD) & (lane < (h + 1) * CA_HD)
        qh = jnp.where(sel, q, 0.0).astype(BF16)
        s = (_dot(qh, kb) if kv_t else _dot_nt(qh, kb)) * (CA_HD ** -0.5)
        e = jnp.exp(s - jnp.max(s, axis=1, keepdims=True))
        p = (e / jnp.sum(e, axis=1, keepdims=True)).astype(BF16)
        out = out + jnp.where(sel, _dot_nt(p, vb) if kv_t else _dot(p, vb), 0.0)
    o_ref[0] = out


def _cross(q3, mem_k, mem_v, tt, kv_t, layer=None):
    b, t, wd = q3.shape
    nm = mem_k.shape[-1] if kv_t else mem_k.shape[-2]
    tt = min(tt, t)
    kv_block = (1, wd, nm) if kv_t else (1, nm, wd)
    if layer is None:
        kv_spec = pl.BlockSpec(kv_block, lambda i, j: (i, 0, 0))
    else:
        kv_spec = pl.BlockSpec((1,) + kv_block, lambda i, j: (layer, i, 0, 0))
    return pl.pallas_call(
        functools.partial(_cross_body, kv_t=kv_t),
        grid=(b, t // tt),
        in_specs=[pl.BlockSpec((1, tt, wd), lambda i, j: (i, j, 0)), kv_spec, kv_spec],
        out_specs=pl.BlockSpec((1, tt, wd), lambda i, j: (i, j, 0)),
        out_shape=jax.ShapeDtypeStruct((b, t, wd), F32),
        compiler_params=_cparams("parallel", "parallel"),
        name="cross_attn",
    )(q3, mem_k, mem_v)


def _prep_layer(p):
    w = {}
    wi = p["w_in"]
    d = wi.shape[0]
    o_i = 4 * 256
    w["w_in"] = jnp.concatenate(
        [wi[:, :o_i], wi[:, o_i + 8:o_i + 8 + 256 + 256 + 128 + 32], wi[:, o_i:o_i + 8],
         jnp.zeros((d, N_IN_PAD - wi.shape[1]), wi.dtype)], axis=1).astype(BF16)
    w["norm_mix_g"] = p["norm_mix_g"]

    g, pn = p["s5_A_re"].shape
    eye = jnp.eye(g, dtype=F32)
    w["s5_are"] = p["s5_A_re"].reshape(1, g * pn)
    w["s5_aim"] = p["s5_A_im"].reshape(1, g * pn)
    w["s5_ldt"] = jnp.repeat(p["s5_log_dt"], pn).reshape(1, g * pn)
    w["s5_br"] = jnp.einsum("gpc,gh->gchp", p["s5_B_re"], eye).reshape(g * S5_CH, g * pn).astype(BF16)
    w["s5_bi"] = jnp.einsum("gpc,gh->gchp", p["s5_B_im"], eye).reshape(g * S5_CH, g * pn).astype(BF16)
    w["s5_cr"] = jnp.einsum("gcp,gh->gphc", p["s5_C_re"], eye).reshape(g * pn, g * S5_CH).astype(BF16)
    w["s5_ci"] = jnp.einsum("gcp,gh->gphc", p["s5_C_im"], eye).reshape(g * pn, g * S5_CH).astype(BF16)
    w["s5_d"] = p["s5_D"].reshape(1, -1)
    w["s5_wglu"] = p["s5_w_glu"].astype(BF16)
    w["s5_go"] = p["s5_out_g"].reshape(1, -1)

    w["ml_cw"] = p["ml_conv_w"]
    w["ml_cb"] = p["ml_conv_b"].reshape(1, -1)
    gate_b = jnp.concatenate([p["ml_b_i"], p["ml_b_f"]])
    w["ml_bcol"] = jnp.zeros((1, 128), F32).at[0, MISC_IG:MISC_IG + 2 * ML_HEADS].set(gate_b)
    w["ml_brow"] = jnp.broadcast_to(gate_b[:, None], (2 * ML_HEADS, 128))
    w["ml_gn"] = p["ml_norm_g"].reshape(1, -1)
    slab_of_col = jnp.arange(2 * ML_HEADS * 128) // 128
    w["ml_psel"] = (jnp.arange(128)[:, None] == (MISC_IG + slab_of_col)[None, :]).astype(BF16)

    wuq = p["mla_w_uq"].reshape(-1, MLA_HEADS, MLA_NOPE + MLA_ROPE)
    w["mla_gq"] = p["mla_q_norm_g"].reshape(1, -1)
    w["mla_wn"] = wuq[:, :, :MLA_NOPE].reshape(wuq.shape[0], -1).astype(BF16)
    w["mla_wr"] = wuq[:, :, MLA_NOPE:].reshape(wuq.shape[0], -1).astype(BF16)
    wuk = p["mla_w_uk"]
    kvl = wuk.shape[0]
    wuk_t = jnp.transpose(wuk, (1, 2, 0)).reshape(MLA_HEADS // 2, 2, MLA_NOPE, kvl)
    eye2 = jnp.eye(2, dtype=F32)
    w["mla_wuk"] = jnp.einsum("jand,ab->janbd", wuk_t, eye2).reshape(MLA_HEADS // 2, 2 * MLA_NOPE, 2 * kvl).astype(BF16)
    w["mla_gkv"] = p["mla_kv_norm_g"].reshape(1, -1)
    src = jnp.arange(MLA_HEADS * MLA_ROPE)
    w["mla_sel"] = ((src[None, :, None] // MLA_ROPE == jnp.arange(MLA_HEADS)[:, None, None])
                    & (src[None, :, None] % MLA_ROPE == jnp.arange(128)[None, None, :])).astype(BF16)
    wuv4 = p["mla_w_uv"].reshape(kvl, MLA_HEADS // 2, 2, MLA_V)
    w["mla_wuv"] = jnp.einsum("cjav,ab->jacbv", wuv4, eye2).reshape(MLA_HEADS // 2, 2 * kvl, 2 * MLA_V).astype(BF16)
    w["mla_go"] = p["mla_out_g"].reshape(1, -1)

    wo = p["w_out"].astype(BF16)
    w["w_out"] = [wo[:256], wo[256:512], wo[512:]]
    w["norm_ca_g"] = p["norm_ca_g"]
    w["ca_mem_g"] = p["ca_mem_g"]
    w["ca_wq"] = p["ca_w_q"].astype(BF16)
    w["ca_wkv"] = jnp.concatenate([p["ca_w_k"], p["ca_w_v"]], axis=1).astype(BF16)
    w["ca_wo"] = p["ca_w_o"].astype(BF16)
    w["norm_ffn_g"] = p["norm_ffn_g"]
    w["ffn_w1"] = p["ffn_w1"].astype(BF16)
    w["ffn_w2"] = p["ffn_w2"].astype(BF16)
    return w


def _rope_tables(pos, reps):
    half = MLA_ROPE // 2
    inv = ROPE_BASE ** (-np.arange(half, dtype=np.float64) * 2.0 / MLA_ROPE)
    ang = pos.astype(np.float64)[:, None] * inv[None, :]
    cos, sin = np.cos(ang).astype(np.float32), np.sin(ang).astype(np.float32)
    zero = np.zeros_like(sin)
    n = 128 // MLA_ROPE
    tabs = (np.tile(np.concatenate([cos, cos], 1), (reps, n)),
            np.tile(np.concatenate([zero, sin], 1), (reps, n)),
            np.tile(np.concatenate([-sin, zero], 1), (reps, n)))
    return tuple(jnp.asarray(t) for t in tabs)


def _layer(x2, b, t, w, tables, mem_k, mem_v, s5_h0, ml_state, conv_buf, paged, final_g, cfg):
    m = b * t
    qw = ML_HEADS * ML_HD
    z = _linear([x2], [w["w_in"]], gamma=w["norm_mix_g"], tm=cfg["tm"])
    z3 = z.reshape(b, t, N_IN_PAD)

    u_t = jnp.transpose(z3[:, :, COL_U:COL_U + 256], (1, 0, 2))
    y_s5_t, s_re, s_im = _s5(u_t, s5_h0[0], s5_h0[1], w, cfg["s5_tc"])
    y_s5 = jnp.transpose(y_s5_t, (1, 0, 2)).reshape(m, 256)

    gates_t = jnp.transpose(z3[:, :, COL_MISC + MISC_IG:COL_MISC + MISC_IG + 2 * ML_HEADS], (0, 2, 1))
    c0, n0, m0 = ml_state
    m0_b = jnp.zeros((b, 8, 128), F32).at[:, :ML_HEADS, :].set(jnp.broadcast_to(m0[:, :, None], (b, ML_HEADS, 128)))
    conv0 = jnp.zeros((b, 8, 2 * qw), F32).at[:, 8 - (ML_CONV - 1):, :].set(conv_buf)
    y_ml, c_pair, m_out = _mlstm(z3, gates_t, conv0, _mlstm_state_in(c0, n0), m0_b, w, cfg["ml_chunk"])
    y_ml = y_ml.reshape(m, qw)
    conv_new = jnp.concatenate([conv_buf, z3[:, :, COL_Q:COL_Q + 2 * qw]], axis=1)[:, t:, :]
    ml_c, ml_n = _mlstm_state_out(c_pair)
    ml_m = m_out[:, :ML_HEADS, 0]

    if paged is None:
        q4, c_lat, kcat, k_rope = _mla_prep(z, tables, b, t, w, cfg["tm"], True)
        kvl = c_lat.shape[1]
        y_mla = _mla_prompt(q4, kcat.reshape(b, t, QK_PAD), w, cfg["tq"])
    else:
        ql, qr, c_lat, kcat, k_rope = _mla_prep(z, tables, b, t, w, cfg["tm"], False)
        kvl = c_lat.shape[1]
        q4 = jnp.concatenate([ql.reshape(b, t, MLA_HEADS, kvl), qr.reshape(b, t, MLA_HEADS, MLA_ROPE),
                              jnp.zeros((b, t, MLA_HEADS, QK_PAD - kvl - MLA_ROPE), BF16)], axis=-1)
        q4 = jnp.transpose(q4, (0, 2, 1, 3))
        cache_c, cache_r, layer, page_table = paged
        y_mla = _mla_sample(q4.reshape(b, MLA_HEADS * t, QK_PAD).astype(F32), kcat.reshape(b, t, QK_PAD), cache_c,
                            cache_r, layer, page_table, w, cfg["pages"])
    y_mla = y_mla.reshape(m, MLA_HEADS * MLA_V)

    x2, qc = _linear([y_s5, y_ml, y_mla], w["w_out"], residual=x2, post=(w["norm_ca_g"], w["ca_wq"]), tm=cfg["tm"])

    oc = _cross(qc.reshape(b, t, -1), mem_k, mem_v, cfg["ca_tt"], cfg["kv_t"], cfg["mem_layer"])

    x2 = _mlp(x2, oc.reshape(m, -1), w["ca_wo"], w["norm_ffn_g"], w["ffn_w1"], w["ffn_w2"],
              w["norm_ffn_g"] if final_g is None else final_g, final_g is not None, tm=cfg["tm"])
    pn = S5_STATE
    states = (s_re.reshape(b, -1, pn), s_im.reshape(b, -1, pn), ml_c, ml_n, ml_m, conv_new,
              c_lat.reshape(b, t, kvl), k_rope.reshape(b, t, MLA_ROPE))
    return x2, states


def kernel(x_prompt, x_sample, state_ssm_re, state_ssm_im, state_mlstm_C, state_mlstm_n, state_mlstm_m,
           state_mlstm_conv, cache_kv_latent, cache_k_rope, cache_mem_k, cache_mem_v, page_table, mem_prompt,
           norm_mix_g, w_in, s5_A_re, s5_A_im, s5_log_dt, s5_B_re, s5_B_im, s5_C_re, s5_C_im, s5_D, s5_w_glu,
           s5_out_g, ml_conv_w, ml_conv_b, ml_b_i, ml_b_f, ml_norm_g, mla_q_norm_g, mla_w_uq, mla_kv_norm_g,
           mla_w_uk, mla_w_uv, mla_out_g, w_out, norm_ca_g, ca_mem_g, ca_w_q, ca_w_k, ca_w_v, ca_w_o,
           norm_ffn_g, ffn_w1, ffn_w2, final_norm_g):
    stacked = dict(norm_mix_g=norm_mix_g, w_in=w_in, s5_A_re=s5_A_re, s5_A_im=s5_A_im, s5_log_dt=s5_log_dt,
                   s5_B_re=s5_B_re, s5_B_im=s5_B_im, s5_C_re=s5_C_re, s5_C_im=s5_C_im, s5_D=s5_D,
                   s5_w_glu=s5_w_glu, s5_out_g=s5_out_g, ml_conv_w=ml_conv_w, ml_conv_b=ml_conv_b, ml_b_i=ml_b_i,
                   ml_b_f=ml_b_f, ml_norm_g=ml_norm_g, mla_q_norm_g=mla_q_norm_g, mla_w_uq=mla_w_uq,
                   mla_kv_norm_g=mla_kv_norm_g, mla_w_uk=mla_w_uk, mla_w_uv=mla_w_uv, mla_out_g=mla_out_g,
                   w_out=w_out, norm_ca_g=norm_ca_g, ca_mem_g=ca_mem_g, ca_w_q=ca_w_q, ca_w_k=ca_w_k,
                   ca_w_v=ca_w_v, ca_w_o=ca_w_o, norm_ffn_g=norm_ffn_g, ffn_w1=ffn_w1, ffn_w2=ffn_w2)
    depth = w_in.shape[0]
    layers = [_prep_layer({k: v[l] for k, v in stacked.items()}) for l in range(depth)]

    bp, tp, d = x_prompt.shape
    bs, ts, _ = x_sample.shape
    gp = s5_A_re.shape[1] * s5_A_re.shape[2]
    qw = ML_HEADS * ML_HD
    n_mem = mem_prompt.shape[1]
    past_len = page_table.shape[1] * cache_kv_latent.shape[2]

    cfg_p = dict(tm=512, s5_tc=64, ml_chunk=128, tq=256, ca_tt=512, pages=1, kv_t=False, mem_layer=None)
    tab_p = _rope_tables(np.arange(tp, dtype=np.int32), 1)
    zero_s5 = (jnp.zeros((bp, gp), F32), jnp.zeros((bp, gp), F32))
    zero_ml = (jnp.zeros((bp, ML_HEADS, ML_HD, ML_HD), F32), jnp.zeros((bp, ML_HEADS, ML_HD), F32),
               jnp.zeros((bp, ML_HEADS), F32))
    zero_conv = jnp.zeros((bp, ML_CONV - 1, 2 * qw), F32)
    xp = x_prompt.reshape(bp * tp, d)
    mem2 = mem_prompt.reshape(bp * n_mem, d)
    p_states = []
    for l, w in enumerate(layers):
        mkv = _linear([mem2], [w["ca_wkv"]], gamma=w["ca_mem_g"], tm=512)
        wd = mkv.shape[1] // 2
        mk = mkv[:, :wd].reshape(bp, n_mem, wd)
        mv = mkv[:, wd:].reshape(bp, n_mem, wd)
        xp, st = _layer(xp, bp, tp, w, tab_p, mk, mv, zero_s5, zero_ml, zero_conv, None,
                        final_norm_g if l == depth - 1 else None, cfg_p)
        p_states.append(st + (mk.reshape(bp, n_mem, CA_HEADS, CA_HD), mv.reshape(bp, n_mem, CA_HEADS, CA_HD)))
    y_prompt = xp.reshape(bp, tp, d)
    p_out = [jnp.stack([s[i] for s in p_states]) for i in range(10)]

    cfg_s = dict(tm=512, s5_tc=ts, ml_chunk=ts, tq=ts, ca_tt=ts, pages=32, kv_t=True)
    tab_s = _rope_tables(past_len + np.arange(ts, dtype=np.int32), bs)
    xs = x_sample.reshape(bs * ts, d)
    cache_rope_t = jnp.swapaxes(cache_k_rope, 2, 3)
    mem_k_t = jnp.transpose(cache_mem_k, (0, 1, 3, 4, 2)).reshape(depth, bs, -1, n_mem)
    mem_v_t = jnp.transpose(cache_mem_v, (0, 1, 3, 4, 2)).reshape(depth, bs, -1, n_mem)
    s_states = []
    for l, w in enumerate(layers):
        cfg_s["mem_layer"] = l
        xs, st = _layer(xs, bs, ts, w, tab_s, mem_k_t, mem_v_t,
                        (state_ssm_re[l].reshape(bs, gp), state_ssm_im[l].reshape(bs, gp)),
                        (state_mlstm_C[l], state_mlstm_n[l], state_mlstm_m[l]), state_mlstm_conv[l],
                        (cache_kv_latent, cache_rope_t, l, page_table),
                        final_norm_g if l == depth - 1 else None, cfg_s)
        s_states.append(st)
    y_sample = xs.reshape(bs, ts, d)
    s_out = [jnp.stack([s[i] for s in s_states]) for i in range(8)]

    return (y_prompt, y_sample, *p_out, *s_out)
```

```python
import functools
import math

import jax
import jax.numpy as jnp
import numpy as np
from jax import lax
from jax.experimental import pallas as pl
from jax.experimental.pallas import tpu as pltpu

F32 = jnp.float32
BF16 = jnp.bfloat16
NORM_EPS = 1e-6
ROPE_BASE = 10000.0

S5_CH = 16
S5_STATE = 64
ML_HEADS = 4
ML_HD = 64
ML_CONV = 4
MLA_HEADS = 8
MLA_NOPE = 64
MLA_ROPE = 32
MLA_V = 64
CA_HEADS = 4
CA_HD = 64
QK_PAD = 256

COL_U, COL_Q, COL_K, COL_V, COL_O, COL_CQ, COL_CKV, COL_MISC = 0, 256, 512, 768, 1024, 1280, 1536, 1664
N_IN_PAD = 1792
MISC_KR, MISC_IG, MISC_FG = 0, 32, 36

VMEM_LIMIT = 56 * 1024 * 1024


def _cparams(*sem):
    return pltpu.CompilerParams(dimension_semantics=sem, vmem_limit_bytes=VMEM_LIMIT)


def _rms(x, g):
    return x * lax.rsqrt(jnp.mean(x * x, axis=-1, keepdims=True) + NORM_EPS) * g


def _dot(a, b):
    return jnp.dot(a, b, preferred_element_type=F32)


def _dot_nt(a, b):
    return lax.dot_general(a, b, (((1,), (1,)), ((), ())), preferred_element_type=F32)


def _dot_tn(a, b):
    return lax.dot_general(a, b, (((0,), (0,)), ((), ())), preferred_element_type=F32)


def _const_spec(shape):
    nd = len(shape)
    return pl.BlockSpec(shape, lambda *_: (0,) * nd)


def _linear_body(*refs, n_in, has_norm, has_res, has_post):
    x_refs = refs[:n_in]
    pos = n_in
    g_ref = refs[pos] if has_norm else None
    pos += int(has_norm)
    w_refs = refs[pos:pos + n_in]
    pos += n_in
    res_ref = refs[pos] if has_res else None
    pos += int(has_res)
    post_refs = refs[pos:pos + 2] if has_post else None
    pos += 2 * int(has_post)
    o_ref = refs[pos]
    acc = None
    for x_ref, w_ref in zip(x_refs, w_refs):
        x = x_ref[...]
        if has_norm:
            x = _rms(x, g_ref[...])
        p = _dot(x.astype(BF16), w_ref[...])
        acc = p if acc is None else acc + p
    if has_res:
        acc = acc + res_ref[...]
    o_ref[...] = acc
    if has_post:
        refs[pos + 1][...] = _dot(_rms(acc, post_refs[0][...]).astype(BF16), post_refs[1][...]).astype(BF16)


def _linear(xs, ws, gamma=None, residual=None, post=None, tm=512):
    m = xs[0].shape[0]
    n = ws[0].shape[1]
    tm = min(tm, m)
    assert m % tm == 0
    in_specs = [pl.BlockSpec((tm, x.shape[1]), lambda i: (i, 0)) for x in xs]
    args = list(xs)
    if gamma is not None:
        in_specs.append(_const_spec((1, xs[0].shape[1])))
        args.append(gamma.reshape(1, -1))
    for w in ws:
        in_specs.append(_const_spec(w.shape))
        args.append(w)
    if residual is not None:
        in_specs.append(pl.BlockSpec((tm, n), lambda i: (i, 0)))
        args.append(residual)
    out_specs = [pl.BlockSpec((tm, n), lambda i: (i, 0))]
    out_shape = [jax.ShapeDtypeStruct((m, n), F32)]
    if post is not None:
        in_specs += [_const_spec((1, n)), _const_spec(post[1].shape)]
        args += [post[0].reshape(1, -1), post[1]]
        out_specs.append(pl.BlockSpec((tm, post[1].shape[1]), lambda i: (i, 0)))
        out_shape.append(jax.ShapeDtypeStruct((m, post[1].shape[1]), BF16))
    body = functools.partial(_linear_body, n_in=len(xs), has_norm=gamma is not None, has_res=residual is not None,
                             has_post=post is not None)
    outs = pl.pallas_call(
        body,
        grid=(m // tm,),
        in_specs=in_specs,
        out_specs=out_specs,
        out_shape=out_shape,
        compiler_params=_cparams("parallel"),
        name="linear",
    )(*args)
    return outs if post is not None else outs[0]


def _mlp_body(x_ref, a_ref, wa_ref, g_ref, w1_ref, w2_ref, gf_ref, o_ref, *, fc, final_norm):
    x = x_ref[...] + _dot(a_ref[...].astype(BF16), wa_ref[...])
    h = _rms(x, g_ref[...]).astype(BF16)
    acc = x
    for c in range(w1_ref.shape[1] // fc):
        a = _dot(h, w1_ref[:, c * fc:(c + 1) * fc])
        a = jnp.maximum(a, 0.0)
        acc = acc + _dot((a * a).astype(BF16), w2_ref[c * fc:(c + 1) * fc, :])
    if final_norm:
        acc = _rms(acc, gf_ref[...])
    o_ref[...] = acc


def _mlp(x, a, wa, gamma, w1, w2, final_gamma, final_norm, tm=512, fc=1024):
    m, d = x.shape
    dff = w1.shape[1]
    tm = min(tm, m)
    fc = min(fc, dff)
    body = functools.partial(_mlp_body, fc=fc, final_norm=final_norm)
    return pl.pallas_call(
        body,
        grid=(m // tm,),
        in_specs=[
            pl.BlockSpec((tm, d), lambda i: (i, 0)),
            pl.BlockSpec((tm, a.shape[1]), lambda i: (i, 0)),
            _const_spec(wa.shape),
            _const_spec((1, d)),
            pl.BlockSpec((d, dff), lambda i: (0, 0), pipeline_mode=pl.Buffered(1)),
            pl.BlockSpec((dff, d), lambda i: (0, 0), pipeline_mode=pl.Buffered(1)),
            _const_spec((1, d)),
        ],
        out_specs=pl.BlockSpec((tm, d), lambda i: (i, 0)),
        out_shape=jax.ShapeDtypeStruct((m, d), F32),
        compiler_params=_cparams("parallel"),
        name="mlp",
    )(x, a, wa, gamma.reshape(1, -1), w1, w2, final_gamma.reshape(1, -1))


def _s5_body(u_ref, h0r_ref, h0i_ref, are_ref, aim_ref, ldt_ref, br_ref, bi_ref, cr_ref, ci_ref, d_ref, wg_ref,
             go_ref, y_ref, hro_ref, hio_ref, xr_s, xi_s, hr_s, hi_s, *, tc, bb):
    @pl.when(pl.program_id(1) == 0)
    def _():
        hr_s[...] = h0r_ref[...]
        hi_s[...] = h0i_ref[...]

    ar = are_ref[...]
    ai = aim_ref[...]
    dt = jnp.exp(ldt_ref[...])
    mag = jnp.exp(ar * dt)
    lr = mag * jnp.cos(ai * dt)
    li = mag * jnp.sin(ai * dt)
    den = ar * ar + ai * ai
    zr = lr - 1.0
    fr = (zr * ar + li * ai) / den
    fi = (li * ar - zr * ai) / den

    ch = u_ref.shape[2]
    u = jnp.swapaxes(u_ref[...], 0, 1).reshape(tc * bb, ch)
    ub = u.astype(BF16)
    pr = _dot(ub, br_ref[...])
    pi = _dot(ub, bi_ref[...])
    xr_s[...] = fr * pr - fi * pi
    xi_s[...] = fr * pi + fi * pr

    def step(t, carry):
        hr, hi = carry
        rows = pl.ds(pl.multiple_of(t * bb, bb), bb)
        nr = lr * hr - li * hi + xr_s[rows, :]
        ni = lr * hi + li * hr + xi_s[rows, :]
        xr_s[rows, :] = nr
        xi_s[rows, :] = ni
        return nr, ni

    hr, hi = lax.fori_loop(0, tc, step, (hr_s[...], hi_s[...]))
    hr_s[...] = hr
    hi_s[...] = hi
    hro_ref[...] = hr
    hio_ref[...] = hi

    y = _dot(xr_s[...].astype(BF16), cr_ref[...]) - _dot(xi_s[...].astype(BF16), ci_ref[...])
    y = jax.nn.gelu(y + d_ref[...] * u)
    g = _dot(y.astype(BF16), wg_ref[...])
    o = g[:, :ch] * jax.nn.sigmoid(g[:, ch:])
    y_ref[...] = jnp.swapaxes(_rms(o, go_ref[...]).reshape(tc, bb, ch), 0, 1).astype(y_ref.dtype)


def _s5(z3, h0r, h0i, w, tc):
    b, t, _ = z3.shape
    ch = w["s5_d"].shape[1]
    gp = h0r.shape[1]
    bb = b if b <= 128 else 128
    tc = min(tc, t)
    body = functools.partial(_s5_body, tc=tc, bb=bb)
    row = lambda n: _const_spec((1, n))
    return pl.pallas_call(
        body,
        grid=(b // bb, t // tc),
        in_specs=[
            pl.BlockSpec((bb, tc, ch), lambda i, j: (i, j, COL_U // ch)),
            pl.BlockSpec((bb, gp), lambda i, j: (i, 0)),
            pl.BlockSpec((bb, gp), lambda i, j: (i, 0)),
            row(gp), row(gp), row(gp),
            _const_spec((ch, gp)), _const_spec((ch, gp)),
            _const_spec((gp, ch)), _const_spec((gp, ch)),
            row(ch), _const_spec((ch, 2 * ch)), row(ch),
        ],
        out_specs=[
            pl.BlockSpec((bb, tc, ch), lambda i, j: (i, j, 0)),
            pl.BlockSpec((bb, gp), lambda i, j: (i, 0)),
            pl.BlockSpec((bb, gp), lambda i, j: (i, 0)),
        ],
        out_shape=[
            jax.ShapeDtypeStruct((b, t, ch), BF16),
            jax.ShapeDtypeStruct((b, gp), F32),
            jax.ShapeDtypeStruct((b, gp), F32),
        ],
        scratch_shapes=[
            pltpu.VMEM((tc * bb, gp), F32), pltpu.VMEM((tc * bb, gp), F32),
            pltpu.VMEM((bb, gp), F32), pltpu.VMEM((bb, gp), F32),
        ],
        compiler_params=_cparams("parallel", "arbitrary"),
        name="s5",
    )(z3, h0r, h0i, w["s5_are"], w["s5_aim"], w["s5_ldt"], w["s5_br"], w["s5_bi"], w["s5_cr"], w["s5_ci"],
      w["s5_d"], w["s5_wglu"], w["s5_go"])


def _split3(x):
    hi = x.astype(BF16)
    r1 = x - hi.astype(F32)
    mid = r1.astype(BF16)
    lo = (r1 - mid.astype(F32)).astype(BF16)
    return hi, mid, lo


def _mlstm_seq(bi, zq_ref, zk_ref, zv_ref, zo_ref, misc_ref, conv0_ref, c0_ref, n0_ref, m0_ref, cw_ref, cb_ref,
               bcol_ref, gn_ref, psel_ref, y_ref, co_ref, no_ref, mo_ref, padq, padk, cs, ms, *, cl):
    hd = ML_HD
    qw = ML_HEADS * hd
    pw = 2 * hd
    blk = (lax.broadcasted_iota(jnp.int32, (pw, pw), 0) // hd) == (lax.broadcasted_iota(jnp.int32, (pw, pw), 1) // hd)
    blk2 = jnp.concatenate([blk, blk], axis=1)

    @pl.when(pl.program_id(1) == 0)
    def _():
        padq[bi, 0:8, :] = conv0_ref[bi][:, :qw]
        padk[bi, 0:8, :] = conv0_ref[bi][:, qw:]
        ms[bi] = m0_ref[bi]
        n_cols = jnp.concatenate([n0_ref[bi], jnp.zeros((pw - 8, pw), F32)], axis=0).T
        zero = jnp.zeros((hd, hd), F32)
        for j in range(ML_HEADS // 2):
            c_blk = jnp.concatenate([jnp.concatenate([c0_ref[bi, 2 * j], zero], axis=1),
                                     jnp.concatenate([zero, c0_ref[bi, 2 * j + 1]], axis=1)], axis=0)
            n_blk = jnp.where(blk, jnp.broadcast_to(n_cols[:, j:j + 1], (pw, pw)), 0.0)
            cs[bi, j] = jnp.concatenate([c_blk, n_blk], axis=1)

    padq[bi, 8:8 + cl, :] = zq_ref[bi]
    padk[bi, 8:8 + cl, :] = zk_ref[bi]
    cw = cw_ref[...]
    cb = cb_ref[...]

    def conv(pad, w, b):
        y = b
        for j in range(ML_CONV):
            y = y + pad[bi, 8 - (ML_CONV - 1) + j:8 - (ML_CONV - 1) + j + cl, :] * w[j:j + 1, :]
        return y

    q = jax.nn.silu(conv(padq, cw[:, :qw], cb[:, :qw]))
    k = jax.nn.silu(conv(padk, cw[:, qw:], cb[:, qw:])) * (hd ** -0.5)
    if cl >= 8:
        padq[bi, 0:8, :] = padq[bi, cl:cl + 8, :]
        padk[bi, 0:8, :] = padk[bi, cl:cl + 8, :]

    gcol = misc_ref[bi] + bcol_ref[...]

    ri = lax.broadcasted_iota(jnp.int32, (cl, cl), 0)
    ci = lax.broadcasted_iota(jnp.int32, (cl, cl), 1)
    tril = ri >= ci
    lower = jnp.where(tril, 1.0, 0.0).astype(BF16)
    bc_col = sum(_dot(lower, part) for part in _split3(jax.nn.log_sigmoid(gcol)))
    lane = lax.broadcasted_iota(jnp.int32, (cl, pw), 1)
    gate_src = jnp.where(lane < MISC_FG, gcol, bc_col)
    rep = sum(_dot(part, psel_ref[...]) for part in _split3(gate_src))
    gate_t = (gate_src if cl == pw else jnp.concatenate([gate_src, jnp.zeros((pw - cl, pw), F32)], axis=0)).T
    key_row = gate_t[MISC_IG:MISC_IG + ML_HEADS, :cl] - gate_t[MISC_FG:MISC_FG + ML_HEADS, :cl]

    first_half = lane < hd
    mean_blk = jnp.where(blk, 1.0 / hd, 0.0).astype(BF16)
    ones_slab = jnp.ones((cl, pw), BF16)
    gn = gn_ref[...]
    for j in range(ML_HEADS // 2):
        ps = slice(j * pw, (j + 1) * pw)
        q2 = q[:, ps]
        k2 = k[:, ps]
        k2b = k2.astype(BF16)
        vo2 = jnp.concatenate([zv_ref[bi][:, ps].astype(BF16), ones_slab], axis=1)
        c_pair = cs[bi, j]
        qc = _dot(q2.astype(BF16), c_pair.astype(BF16))
        hh2 = w_s2 = decay2 = None
        for a in range(2):
            h = 2 * j + a
            ig_rep = rep[:, h * pw:(h + 1) * pw]
            bc_rep = rep[:, (ML_HEADS + h) * pw:(ML_HEADS + h + 1) * pw]
            m_prev = ms[bi, h:h + 1, :]
            d = jnp.where(tril, bc_rep[:, :cl] + key_row[h:h + 1, :], -jnp.inf)
            inter = bc_rep + m_prev
            m_tok = jnp.maximum(inter, jnp.max(d, axis=1, keepdims=True))
            w_intra = jnp.exp(d - m_tok[:, :cl])
            w_inter = jnp.exp(inter - m_tok)
            mine = first_half if a == 0 else jnp.logical_not(first_half)
            s = _dot_nt(jnp.where(mine, q2, 0.0).astype(BF16), k2b) * w_intra
            nd = _dot(s.astype(BF16), vo2)
            num = nd[:, :pw] + w_inter * qc[:, :pw]
            den = nd[:, pw:] + w_inter * qc[:, pw:]
            hh = num / jnp.maximum(jnp.abs(den), jnp.exp(-m_tok))
            m_end = m_tok[cl - 1:cl, :]
            g_end = bc_rep[cl - 1:cl, :]
            w_s = jnp.exp(g_end - bc_rep + ig_rep - m_end)
            decay = jnp.exp(g_end + m_prev - m_end)
            ms[bi, h:h + 1, :] = m_end
            if a == 0:
                hh2, w_s2, decay2 = hh, w_s, decay
            else:
                hh2 = jnp.where(first_half, hh2, hh)
                w_s2 = jnp.where(first_half, w_s2, w_s)
                decay2 = jnp.where(first_half[0:1, :], decay2, decay)
        upd = _dot_tn((k2 * w_s2).astype(BF16), vo2)
        cs[bi, j] = jnp.concatenate([decay2, decay2], axis=1) * c_pair + jnp.where(blk2, upd, 0.0)
        oh = jax.nn.sigmoid(zo_ref[bi][:, ps]) * hh2
        sq = oh * oh
        sq_hi = sq.astype(BF16)
        msq = _dot(sq_hi, mean_blk) + _dot((sq - sq_hi.astype(F32)).astype(BF16), mean_blk)
        y_ref[bi, :, ps] = (oh * lax.rsqrt(msq + NORM_EPS) * gn[:, ps]).astype(y_ref.dtype)

    @pl.when(pl.program_id(1) == pl.num_programs(1) - 1)
    def _():
        mo_ref[bi] = ms[bi]
        no_ref[bi] = jnp.zeros(no_ref.shape[1:], F32)
        for j in range(ML_HEADS // 2):
            st = cs[bi, j]
            for a in range(2):
                co_ref[bi, 2 * j + a] = st[a * hd:(a + 1) * hd, a * hd:(a + 1) * hd]
            n_t = st[:, pw:].T
            no_ref[bi, j:j + 1, :] = n_t[0:1, :] + n_t[hd:hd + 1, :]


def _mlstm_body(*refs, cl, bb):
    for bi in range(bb):
        _mlstm_seq(bi, *refs, cl=cl)


def _mlstm(z3, conv0, c0, n0, m0, w, cl, bb):
    b, t, _ = z3.shape
    qw = ML_HEADS * ML_HD
    cl = min(cl, t)
    assert cl <= 2 * ML_HD and b % bb == 0
    body = functools.partial(_mlstm_body, cl=cl, bb=bb)
    zspec = lambda col: pl.BlockSpec((bb, cl, qw), lambda i, j: (i, j, col // qw))
    c_spec = pl.BlockSpec((bb, ML_HEADS, ML_HD, ML_HD), lambda i, j: (i, 0, 0, 0))
    row_spec = pl.BlockSpec((bb, 8, 128), lambda i, j: (i, 0, 0))
    return pl.pallas_call(
        body,
        grid=(b // bb, t // cl),
        in_specs=[
            zspec(COL_Q), zspec(COL_K), zspec(COL_V), zspec(COL_O),
            pl.BlockSpec((bb, cl, 128), lambda i, j: (i, j, COL_MISC // 128)),
            pl.BlockSpec((bb, 8, 2 * qw), lambda i, j: (i, 0, 0)),
            c_spec, row_spec, row_spec,
            _const_spec((ML_CONV, 2 * qw)), _const_spec((1, 2 * qw)),
            _const_spec((1, 128)), _const_spec((1, qw)), _const_spec(w["ml_psel"].shape),
        ],
        out_specs=[pl.BlockSpec((bb, cl, qw), lambda i, j: (i, j, 0)), c_spec, row_spec, row_spec],
        out_shape=[
            jax.ShapeDtypeStruct((b, t, qw), BF16),
            jax.ShapeDtypeStruct((b, ML_HEADS, ML_HD, ML_HD), F32),
            jax.ShapeDtypeStruct((b, 8, 128), F32),
            jax.ShapeDtypeStruct((b, 8, 128), F32),
        ],
        scratch_shapes=[
            pltpu.VMEM((bb, cl + 8, qw), F32), pltpu.VMEM((bb, cl + 8, qw), F32),
            pltpu.VMEM((bb, ML_HEADS // 2, 2 * ML_HD, 4 * ML_HD), F32), pltpu.VMEM((bb, 8, 128), F32),
        ],
        compiler_params=_cparams("parallel", "arbitrary"),
        name="mlstm",
    )(z3, z3, z3, z3, z3, conv0, c0, n0, m0, w["ml_cw"], w["ml_cb"], w["ml_bcol"], w["ml_gn"], w["ml_psel"])


def _rope128(x, cos_t, sin_up, sin_dn):
    half = MLA_ROPE // 2
    return x * cos_t + pltpu.roll(x, half, 1) * sin_up + pltpu.roll(x, 128 - half, 1) * sin_dn


def _mla_prep_body(zcq_ref, zckv_ref, misc_ref, cos_ref, sup_ref, sdn_ref, gq_ref, wn_ref, wr_ref, wuk_ref, gkv_ref,
                   sel_ref, *out_refs, heads_major):
    cos_t, sup, sdn = cos_ref[...], sup_ref[...], sdn_ref[...]
    cq = _rms(zcq_ref[...], gq_ref[...]).astype(BF16)
    qn = _dot(cq, wn_ref[...])
    qrp = _dot(cq, wr_ref[...])
    ql = [_dot(qn[:, 128 * j:128 * (j + 1)].astype(BF16), wuk_ref[j]).astype(BF16) for j in range(wuk_ref.shape[0])]
    qr = [_rope128(qrp[:, 128 * j:128 * (j + 1)], cos_t, sup, sdn).astype(BF16) for j in range(qrp.shape[1] // 128)]
    if heads_major:
        q_ref, c_ref, kcat_ref, kr_ref = out_refs
        kvl = ql[0].shape[1] // 2
        qr_all = jnp.concatenate(qr, axis=1)
        for h in range(MLA_HEADS):
            q_ref[0, h, :, 0:kvl] = ql[h // 2][:, (h % 2) * kvl:(h % 2 + 1) * kvl]
            q_ref[0, h, :, kvl:QK_PAD] = _dot(qr_all, sel_ref[h]).astype(BF16)
    else:
        ql_ref, qr_ref, c_ref, kcat_ref, kr_ref = out_refs
        for j, v in enumerate(ql):
            ql_ref[:, 256 * j:256 * (j + 1)] = v
        for j, v in enumerate(qr):
            qr_ref[:, 128 * j:128 * (j + 1)] = v
    c = _rms(zckv_ref[...], gkv_ref[...])
    c_ref[...] = c
    krf = _rope128(misc_ref[...], cos_t, sup, sdn)
    kr_ref[...] = krf[:, MISC_KR:MISC_KR + MLA_ROPE]
    lane = lax.broadcasted_iota(jnp.int32, krf.shape, 1)
    kcat_ref[:, 0:128] = c.astype(BF16)
    kcat_ref[:, 128:256] = jnp.where(lane < MLA_ROPE, krf, 0.0).astype(BF16)


def _mla_prep(z, tables, b, t_len, w, tm, heads_major):
    m = z.shape[0]
    tm = min(tm, m, t_len) if heads_major else m
    nt = tables[0].shape[0] // tm
    tspec = pl.BlockSpec((tm, 128), lambda i: (i % nt, 0))
    kvl = 128
    if heads_major:
        q_specs = [pl.BlockSpec((1, MLA_HEADS, tm, QK_PAD), lambda i: (i // nt, 0, i % nt, 0))]
        q_shapes = [jax.ShapeDtypeStruct((b, MLA_HEADS, t_len, QK_PAD), BF16)]
    else:
        q_specs = [pl.BlockSpec((tm, MLA_HEADS * kvl), lambda i: (i, 0)),
                   pl.BlockSpec((tm, MLA_HEADS * MLA_ROPE), lambda i: (i, 0))]
        q_shapes = [jax.ShapeDtypeStruct((m, MLA_HEADS * kvl), BF16),
                    jax.ShapeDtypeStruct((m, MLA_HEADS * MLA_ROPE), BF16)]
    return pl.pallas_call(
        functools.partial(_mla_prep_body, heads_major=heads_major),
        grid=(m // tm,),
        in_specs=[
            pl.BlockSpec((tm, 256), lambda i: (i, COL_CQ // 256)),
            pl.BlockSpec((tm, kvl), lambda i: (i, COL_CKV // 128)),
            pl.BlockSpec((tm, 128), lambda i: (i, COL_MISC // 128)),
            tspec, tspec, tspec,
            _const_spec((1, 256)), _const_spec(w["mla_wn"].shape), _const_spec(w["mla_wr"].shape),
            _const_spec(w["mla_wuk"].shape), _const_spec((1, kvl)), _const_spec(w["mla_sel"].shape),
        ],
        out_specs=q_specs + [
            pl.BlockSpec((tm, kvl), lambda i: (i, 0)),
            pl.BlockSpec((tm, QK_PAD), lambda i: (i, 0)),
            pl.BlockSpec((tm, MLA_ROPE), lambda i: (i, 0)),
        ],
        out_shape=q_shapes + [
            jax.ShapeDtypeStruct((m, kvl), F32),
            jax.ShapeDtypeStruct((m, QK_PAD), BF16),
            jax.ShapeDtypeStruct((m, MLA_ROPE), F32),
        ],
        compiler_params=_cparams("parallel"),
        name="mla_prep",
    )(z, z, z, tables[0], tables[1], tables[2], w["mla_gq"], w["mla_wn"], w["mla_wr"], w["mla_wuk"], w["mla_gkv"],
      w["mla_sel"])


def _mla_out(o_heads, wuv_ref, g_ref):
    ys = []
    for j in range(MLA_HEADS // 2):
        pair = jnp.concatenate([o_heads[2 * j], o_heads[2 * j + 1]], axis=1).astype(BF16)
        ys.append(_dot(pair, wuv_ref[j]))
    return _rms(jnp.concatenate(ys, axis=1), g_ref[...])


def _mla_prompt_body(q_ref, k_ref, wuv_ref, g_ref, o_ref, m_s, l_s, acc_s, *, tq, scale):
    i = pl.program_id(1)
    kvl = acc_s.shape[1]
    scale_log2e = scale * math.log2(math.e)
    m_s[...] = jnp.full(m_s.shape, -jnp.inf, F32)
    l_s[...] = jnp.zeros(l_s.shape, F32)
    acc_s[...] = jnp.zeros(acc_s.shape, F32)

    def block(j, masked):
        kb = k_ref[0, pl.ds(pl.multiple_of(j * tq, tq), tq), :]
        vb = kb[:, :kvl]
        if masked:
            causal = (lax.broadcasted_iota(jnp.int32, (tq, tq), 1) <= lax.broadcasted_iota(jnp.int32, (tq, tq), 0))
        for h in range(MLA_HEADS):
            rows = slice(h * tq, (h + 1) * tq)
            s = _dot_nt(q_ref[0, h], kb)
            if masked:
                s = jnp.where(causal, s, -jnp.inf)
            m_prev = m_s[rows, :]
            m_next = jnp.maximum(m_prev, jnp.max(s, axis=1, keepdims=True))
            alpha = jnp.exp2((m_prev - m_next) * scale_log2e)
            p = jnp.exp2((s - jnp.concatenate([m_next] * (tq // kvl), axis=1)) * scale_log2e)
            p_lanes = p[:, 0:kvl]
            for c in range(1, tq // kvl):
                p_lanes = p_lanes + p[:, c * kvl:(c + 1) * kvl]
            l_s[rows, :] = alpha * l_s[rows, :] + p_lanes
            acc_s[rows, :] = alpha * acc_s[rows, :] + _dot(p.astype(BF16), vb)
            m_s[rows, :] = m_next

    def full_block(j, carry):
        block(j, False)
        return carry

    lax.fori_loop(0, i, full_block, 0)
    block(i, True)
    o = acc_s[...] / jnp.sum(l_s[...], axis=1, keepdims=True)
    o_ref[0] = _mla_out([o[h * tq:(h + 1) * tq] for h in range(MLA_HEADS)], wuv_ref, g_ref).astype(o_ref.dtype)


def _mla_prompt(q, kcat, w, tq):
    b, hn, t, qk = q.shape
    tq = min(tq, t)
    kvl = 128
    wo = MLA_HEADS * MLA_V
    body = functools.partial(_mla_prompt_body, tq=tq, scale=1.0 / math.sqrt(MLA_NOPE + MLA_ROPE))
    return pl.pallas_call(
        body,
        grid=(b, t // tq),
        in_specs=[
            pl.BlockSpec((1, hn, tq, qk), lambda i, j: (i, 0, j, 0)),
            pl.BlockSpec((1, t, qk), lambda i, j: (i, 0, 0)),
            _const_spec(w["mla_wuv"].shape), _const_spec((1, wo)),
        ],
        out_specs=pl.BlockSpec((1, tq, wo), lambda i, j: (i, j, 0)),
        out_shape=jax.ShapeDtypeStruct((b, t, wo), BF16),
        scratch_shapes=[pltpu.VMEM((hn * tq, kvl), F32), pltpu.VMEM((hn * tq, kvl), F32),
                        pltpu.VMEM((hn * tq, kvl), F32)],
        compiler_params=_cparams("parallel", "parallel"),
        name="mla_prompt",
    )(q, kcat, w["mla_wuv"], w["mla_go"])


def _mla_sample_body(pt_ref, q_ref, kn_ref, cc_hbm, cr_hbm, wuv_ref, g_ref, o_ref, cbuf, rbuf, sem, m_s, l_s, acc_s,
                     *, layer, group, n_groups, t_new, scale):
    b = pl.program_id(0)
    kvl = acc_s.shape[1]
    page = cbuf.shape[1] // group

    def group_copies(bi, g, slot):
        cps = []
        for i in range(group):
            pid = pt_ref[bi, g * group + i]
            cps.append(pltpu.make_async_copy(cc_hbm.at[layer, pid], cbuf.at[slot, pl.ds(i * page, page), :],
                                             sem.at[slot]))
            cps.append(pltpu.make_async_copy(cr_hbm.at[layer, pid], rbuf.at[slot, :, pl.ds(i * page, page)],
                                             sem.at[slot]))
        return cps

    def start_group(bi, g, slot):
        for cp in group_copies(bi, g, slot):
            cp.start()

    def wait_group(bi, g, slot):
        for cp in group_copies(bi, g, slot):
            cp.wait()

    @pl.when(b == 0)
    def _():
        start_group(0, 0, 0)

    m_s[...] = jnp.full(m_s.shape, -jnp.inf, F32)
    l_s[...] = jnp.zeros(l_s.shape, F32)
    acc_s[...] = jnp.zeros(acc_s.shape, F32)
    q = q_ref[0].astype(BF16)
    q_lat = q[:, 0:kvl]
    q_rope = q[:, kvl:kvl + MLA_ROPE]

    first = lax.rem(b * n_groups, 2)
    for g in range(n_groups):
        slot = lax.rem(first + g, 2)
        if g + 1 < n_groups:
            start_group(b, g + 1, 1 - slot)
        else:
            @pl.when(b + 1 < pl.num_programs(0))
            def _():
                start_group(b + 1, 0, 1 - slot)
        wait_group(b, g, slot)
        cb = cbuf[slot].astype(BF16)
        rb = rbuf[slot].astype(BF16)
        s = (_dot_nt(q_lat, cb) + _dot(q_rope, rb)) * scale
        m_old = m_s[...]
        m_new = jnp.maximum(m_old, jnp.max(s, axis=1, keepdims=True))
        alpha = jnp.exp(m_old - m_new)
        p = jnp.exp(s - m_new)
        l_s[...] = alpha * l_s[...] + jnp.sum(p, axis=1, keepdims=True)
        acc_s[...] = alpha * acc_s[...] + _dot(p.astype(BF16), cb)
        m_s[...] = m_new

    kn = kn_ref[0].astype(F32)
    qf = q.astype(F32)
    t_row = lax.broadcasted_iota(jnp.int32, (q.shape[0], 1), 0) % t_new
    s_new = []
    for t2 in range(t_new):
        st = jnp.sum(qf * kn[t2:t2 + 1, :], axis=1, keepdims=True) * scale
        s_new.append(jnp.where(t_row >= t2, st, -jnp.inf))
    m_o = m_s[...]
    m_n = m_o
    for st in s_new:
        m_n = jnp.maximum(m_n, st)
    al = jnp.exp(m_o - m_n)
    l = al * l_s[...]
    acc = al * acc_s[...]
    for t2, st in enumerate(s_new):
        pt = jnp.exp(st - m_n)
        l = l + pt
        acc = acc + pt * kn[t2:t2 + 1, 0:kvl]
    acc_s[...] = acc / l
    o_ref[0] = _mla_out([acc_s[h * t_new:(h + 1) * t_new, :] for h in range(MLA_HEADS)], wuv_ref,
                        g_ref).astype(o_ref.dtype)


def _mla_sample(q, knew, cache_c, cache_r, layer, page_table, w, pages):
    b, rows, qk = q.shape
    t_new = knew.shape[1]
    n_pages = page_table.shape[1]
    page, kvl = cache_c.shape[2], cache_c.shape[3]
    group = min(pages, n_pages)
    assert n_pages % group == 0
    wo = MLA_HEADS * MLA_V
    body = functools.partial(_mla_sample_body, layer=layer, group=group, n_groups=n_pages // group, t_new=t_new,
                             scale=1.0 / math.sqrt(MLA_NOPE + MLA_ROPE))
    grid_spec = pltpu.PrefetchScalarGridSpec(
        num_scalar_prefetch=1,
        grid=(b,),
        in_specs=[
            pl.BlockSpec((1, rows, qk), lambda bi, pt: (bi, 0, 0)),
            pl.BlockSpec((1, t_new, qk), lambda bi, pt: (bi, 0, 0)),
            pl.BlockSpec(memory_space=pl.ANY),
            pl.BlockSpec(memory_space=pl.ANY),
            pl.BlockSpec(w["mla_wuv"].shape, lambda bi, pt: (0, 0, 0)),
            pl.BlockSpec((1, wo), lambda bi, pt: (0, 0)),
        ],
        out_specs=pl.BlockSpec((1, t_new, wo), lambda bi, pt: (bi, 0, 0)),
        scratch_shapes=[
            pltpu.VMEM((2, group * page, kvl), F32),
            pltpu.VMEM((2, MLA_ROPE, group * page), F32),
            pltpu.SemaphoreType.DMA((2,)),
            pltpu.VMEM((rows, 1), F32), pltpu.VMEM((rows, 1), F32), pltpu.VMEM((rows, kvl), F32),
        ],
    )
    return pl.pallas_call(
        body,
        grid_spec=grid_spec,
        out_shape=jax.ShapeDtypeStruct((b, t_new, wo), BF16),
        compiler_params=_cparams("arbitrary"),
        name="mla_sample",
    )(page_table, q, knew, cache_c, cache_r, w["mla_wuv"], w["mla_go"])


def _cross_body(q_ref, k_ref, v_ref, o_ref, *, kv_t, bb):
    k4 = len(k_ref.shape) == 4
    for bi in range(bb):
        q = q_ref[bi].astype(F32)
        kb = (k_ref[0, bi] if k4 else k_ref[bi]).astype(BF16)
        vb = (v_ref[0, bi] if k4 else v_ref[bi]).astype(BF16)
        lane = lax.broadcasted_iota(jnp.int32, q.shape, 1)
        out = jnp.zeros(q.shape, F32)
        for h in range(CA_HEADS):
            sel = (lane >= h * CA_HD) & (lane < (h + 1) * CA_HD)
            qh = jnp.where(sel, q, 0.0).astype(BF16)
            s = (_dot(qh, kb) if kv_t else _dot_nt(qh, kb)) * (CA_HD ** -0.5)
            e = jnp.exp(s - jnp.max(s, axis=1, keepdims=True))
            p = (e / jnp.sum(e, axis=1, keepdims=True)).astype(BF16)
            out = out + jnp.where(sel, _dot_nt(p, vb) if kv_t else _dot(p, vb), 0.0)
        o_ref[bi] = out.astype(o_ref.dtype)


def _cross(q3, mem_k, mem_v, tt, kv_t, layer=None, bb=1):
    b, t, wd = q3.shape
    nm = mem_k.shape[-1] if kv_t else mem_k.shape[-2]
    tt = min(tt, t)
    assert b % bb == 0
    kv_block = (bb, wd, nm) if kv_t else (bb, nm, wd)
    if layer is None:
        kv_spec = pl.BlockSpec(kv_block, lambda i, j: (i, 0, 0))
    else:
        kv_spec = pl.BlockSpec((1,) + kv_block, lambda i, j: (layer, i, 0, 0))
    return pl.pallas_call(
        functools.partial(_cross_body, kv_t=kv_t, bb=bb),
        grid=(b // bb, t // tt),
        in_specs=[pl.BlockSpec((bb, tt, wd), lambda i, j: (i, j, 0)), kv_spec, kv_spec],
        out_specs=pl.BlockSpec((bb, tt, wd), lambda i, j: (i, j, 0)),
        out_shape=jax.ShapeDtypeStruct((b, t, wd), BF16),
        compiler_params=_cparams("parallel", "parallel"),
        name="cross_attn",
    )(q3, mem_k, mem_v)


def _prep_layer(p):
    w = {}
    wi = p["w_in"]
    d = wi.shape[0]
    o_i = 4 * 256
    w["w_in"] = jnp.concatenate(
        [wi[:, :o_i], wi[:, o_i + 8:o_i + 8 + 256 + 256 + 128 + 32], wi[:, o_i:o_i + 8],
         jnp.zeros((d, N_IN_PAD - wi.shape[1]), wi.dtype)], axis=1).astype(BF16)
    w["norm_mix_g"] = p["norm_mix_g"]

    g, pn = p["s5_A_re"].shape
    eye = jnp.eye(g, dtype=F32)
    w["s5_are"] = p["s5_A_re"].reshape(1, g * pn)
    w["s5_aim"] = p["s5_A_im"].reshape(1, g * pn)
    w["s5_ldt"] = jnp.repeat(p["s5_log_dt"], pn).reshape(1, g * pn)
    w["s5_br"] = jnp.einsum("gpc,gh->gchp", p["s5_B_re"], eye).reshape(g * S5_CH, g * pn).astype(BF16)
    w["s5_bi"] = jnp.einsum("gpc,gh->gchp", p["s5_B_im"], eye).reshape(g * S5_CH, g * pn).astype(BF16)
    w["s5_cr"] = jnp.einsum("gcp,gh->gphc", p["s5_C_re"], eye).reshape(g * pn, g * S5_CH).astype(BF16)
    w["s5_ci"] = jnp.einsum("gcp,gh->gphc", p["s5_C_im"], eye).reshape(g * pn, g * S5_CH).astype(BF16)
    w["s5_d"] = p["s5_D"].reshape(1, -1)
    w["s5_wglu"] = p["s5_w_glu"].astype(BF16)
    w["s5_go"] = p["s5_out_g"].reshape(1, -1)

    w["ml_cw"] = p["ml_conv_w"]
    w["ml_cb"] = p["ml_conv_b"].reshape(1, -1)
    gate_b = jnp.concatenate([p["ml_b_i"], p["ml_b_f"]])
    w["ml_bcol"] = jnp.zeros((1, 128), F32).at[0, MISC_IG:MISC_IG + 2 * ML_HEADS].set(gate_b)
    w["ml_gn"] = p["ml_norm_g"].reshape(1, -1)
    slab_of_col = jnp.arange(2 * ML_HEADS * 128) // 128
    w["ml_psel"] = (jnp.arange(128)[:, None] == (MISC_IG + slab_of_col)[None, :]).astype(BF16)

    wuq = p["mla_w_uq"].reshape(-1, MLA_HEADS, MLA_NOPE + MLA_ROPE)
    w["mla_gq"] = p["mla_q_norm_g"].reshape(1, -1)
    w["mla_wn"] = wuq[:, :, :MLA_NOPE].reshape(wuq.shape[0], -1).astype(BF16)
    w["mla_wr"] = wuq[:, :, MLA_NOPE:].reshape(wuq.shape[0], -1).astype(BF16)
    wuk = p["mla_w_uk"]
    kvl = wuk.shape[0]
    wuk_t = jnp.transpose(wuk, (1, 2, 0)).reshape(MLA_HEADS // 2, 2, MLA_NOPE, kvl)
    eye2 = jnp.eye(2, dtype=F32)
    w["mla_wuk"] = jnp.einsum("jand,ab->janbd", wuk_t, eye2).reshape(MLA_HEADS // 2, 2 * MLA_NOPE, 2 * kvl).astype(BF16)
    w["mla_gkv"] = p["mla_kv_norm_g"].reshape(1, -1)
    src = jnp.arange(MLA_HEADS * MLA_ROPE)
    w["mla_sel"] = ((src[None, :, None] // MLA_ROPE == jnp.arange(MLA_HEADS)[:, None, None])
                    & (src[None, :, None] % MLA_ROPE == jnp.arange(128)[None, None, :])).astype(BF16)
    wuv4 = p["mla_w_uv"].reshape(kvl, MLA_HEADS // 2, 2, MLA_V)
    w["mla_wuv"] = jnp.einsum("cjav,ab->jacbv", wuv4, eye2).reshape(MLA_HEADS // 2, 2 * kvl, 2 * MLA_V).astype(BF16)
    w["mla_go"] = p["mla_out_g"].reshape(1, -1)

    wo = p["w_out"].astype(BF16)
    w["w_out"] = [wo[:256], wo[256:512], wo[512:]]
    w["norm_ca_g"] = p["norm_ca_g"]
    w["ca_mem_g"] = p["ca_mem_g"]
    w["ca_wq"] = p["ca_w_q"].astype(BF16)
    w["ca_wkv"] = jnp.concatenate([p["ca_w_k"], p["ca_w_v"]], axis=1).astype(BF16)
    w["ca_wo"] = p["ca_w_o"].astype(BF16)
    w["norm_ffn_g"] = p["norm_ffn_g"]
    w["ffn_w1"] = p["ffn_w1"].astype(BF16)
    w["ffn_w2"] = p["ffn_w2"].astype(BF16)
    return w


def _rope_tables(pos, reps):
    half = MLA_ROPE // 2
    inv = ROPE_BASE ** (-np.arange(half, dtype=np.float64) * 2.0 / MLA_ROPE)
    ang = pos.astype(np.float64)[:, None] * inv[None, :]
    cos, sin = np.cos(ang).astype(np.float32), np.sin(ang).astype(np.float32)
    zero = np.zeros_like(sin)
    n = 128 // MLA_ROPE
    tabs = (np.tile(np.concatenate([cos, cos], 1), (reps, n)),
            np.tile(np.concatenate([zero, sin], 1), (reps, n)),
            np.tile(np.concatenate([-sin, zero], 1), (reps, n)))
    return tuple(jnp.asarray(t) for t in tabs)


def _layer(x2, b, t, w, tables, mem_k, mem_v, s5_h0, ml_state, conv_buf, paged, final_g, cfg):
    m = b * t
    qw = ML_HEADS * ML_HD
    z = _linear([x2], [w["w_in"]], gamma=w["norm_mix_g"], tm=cfg["tm"])
    z3 = z.reshape(b, t, N_IN_PAD)

    y_s5, s_re, s_im = _s5(z3, s5_h0[0], s5_h0[1], w, cfg["s5_tc"])
    y_s5 = y_s5.reshape(m, -1)

    c0, n0, m0 = ml_state
    m0_b = jnp.zeros((b, 8, 128), F32).at[:, :ML_HEADS, :].set(jnp.broadcast_to(m0[:, :, None], (b, ML_HEADS, 128)))
    n0_b = jnp.zeros((b, 8, 2 * ML_HD), F32).at[:, :ML_HEADS // 2, :].set(n0.reshape(b, ML_HEADS // 2, 2 * ML_HD))
    conv0 = jnp.zeros((b, 8, 2 * qw), F32).at[:, 8 - (ML_CONV - 1):, :].set(conv_buf)
    y_ml, ml_c, n_out, m_out = _mlstm(z3, conv0, c0, n0_b, m0_b, w, cfg["ml_chunk"], cfg["seqs_per_step"])
    y_ml = y_ml.reshape(m, qw)
    conv_new = jnp.concatenate([conv_buf, z3[:, :, COL_Q:COL_Q + 2 * qw]], axis=1)[:, t:, :]
    ml_n = n_out[:, :ML_HEADS // 2, :].reshape(b, ML_HEADS, ML_HD)
    ml_m = m_out[:, :ML_HEADS, 0]

    if paged is None:
        q4, c_lat, kcat, k_rope = _mla_prep(z, tables, b, t, w, cfg["tm"], True)
        kvl = c_lat.shape[1]
        y_mla = _mla_prompt(q4, kcat.reshape(b, t, QK_PAD), w, cfg["tq"])
    else:
        ql, qr, c_lat, kcat, k_rope = _mla_prep(z, tables, b, t, w, cfg["tm"], False)
        kvl = c_lat.shape[1]
        q4 = jnp.concatenate([ql.reshape(b, t, MLA_HEADS, kvl), qr.reshape(b, t, MLA_HEADS, MLA_ROPE),
                              jnp.zeros((b, t, MLA_HEADS, QK_PAD - kvl - MLA_ROPE), BF16)], axis=-1)
        q4 = jnp.transpose(q4, (0, 2, 1, 3))
        cache_c, cache_r, layer, page_table = paged
        y_mla = _mla_sample(q4.reshape(b, MLA_HEADS * t, QK_PAD).astype(F32), kcat.reshape(b, t, QK_PAD), cache_c,
                            cache_r, layer, page_table, w, cfg["pages"])
    y_mla = y_mla.reshape(m, MLA_HEADS * MLA_V)

    x2, qc = _linear([y_s5, y_ml, y_mla], w["w_out"], residual=x2, post=(w["norm_ca_g"], w["ca_wq"]), tm=cfg["tm"])

    oc = _cross(qc.reshape(b, t, -1), mem_k, mem_v, cfg["ca_tt"], cfg["kv_t"], cfg["mem_layer"], cfg["seqs_per_step"])

    x2 = _mlp(x2, oc.reshape(m, -1), w["ca_wo"], w["norm_ffn_g"], w["ffn_w1"], w["ffn_w2"],
              w["norm_ffn_g"] if final_g is None else final_g, final_g is not None, tm=cfg["tm"])
    pn = S5_STATE
    states = (s_re.reshape(b, -1, pn), s_im.reshape(b, -1, pn), ml_c, ml_n, ml_m, conv_new,
              c_lat.reshape(b, t, kvl), k_rope.reshape(b, t, MLA_ROPE))
    return x2, states


def kernel(x_prompt, x_sample, state_ssm_re, state_ssm_im, state_mlstm_C, state_mlstm_n, state_mlstm_m,
           state_mlstm_conv, cache_kv_latent, cache_k_rope, cache_mem_k, cache_mem_v, page_table, mem_prompt,
           norm_mix_g, w_in, s5_A_re, s5_A_im, s5_log_dt, s5_B_re, s5_B_im, s5_C_re, s5_C_im, s5_D, s5_w_glu,
           s5_out_g, ml_conv_w, ml_conv_b, ml_b_i, ml_b_f, ml_norm_g, mla_q_norm_g, mla_w_uq, mla_kv_norm_g,
           mla_w_uk, mla_w_uv, mla_out_g, w_out, norm_ca_g, ca_mem_g, ca_w_q, ca_w_k, ca_w_v, ca_w_o,
           norm_ffn_g, ffn_w1, ffn_w2, final_norm_g):
    stacked = dict(norm_mix_g=norm_mix_g, w_in=w_in, s5_A_re=s5_A_re, s5_A_im=s5_A_im, s5_log_dt=s5_log_dt,
                   s5_B_re=s5_B_re, s5_B_im=s5_B_im, s5_C_re=s5_C_re, s5_C_im=s5_C_im, s5_D=s5_D,
                   s5_w_glu=s5_w_glu, s5_out_g=s5_out_g, ml_conv_w=ml_conv_w, ml_conv_b=ml_conv_b, ml_b_i=ml_b_i,
                   ml_b_f=ml_b_f, ml_norm_g=ml_norm_g, mla_q_norm_g=mla_q_norm_g, mla_w_uq=mla_w_uq,
                   mla_kv_norm_g=mla_kv_norm_g, mla_w_uk=mla_w_uk, mla_w_uv=mla_w_uv, mla_out_g=mla_out_g,
                   w_out=w_out, norm_ca_g=norm_ca_g, ca_mem_g=ca_mem_g, ca_w_q=ca_w_q, ca_w_k=ca_w_k,
                   ca_w_v=ca_w_v, ca_w_o=ca_w_o, norm_ffn_g=norm_ffn_g, ffn_w1=ffn_w1, ffn_w2=ffn_w2)
    depth = w_in.shape[0]
    layers = [_prep_layer({k: v[l] for k, v in stacked.items()}) for l in range(depth)]

    bp, tp, d = x_prompt.shape
    bs, ts, _ = x_sample.shape
    gp = s5_A_re.shape[1] * s5_A_re.shape[2]
    qw = ML_HEADS * ML_HD
    n_mem = mem_prompt.shape[1]
    past_len = page_table.shape[1] * cache_kv_latent.shape[2]

    cfg_p = dict(tm=512, s5_tc=64, ml_chunk=128, seqs_per_step=1, tq=256, ca_tt=512, pages=1, kv_t=False, mem_layer=None)
    tab_p = _rope_tables(np.arange(tp, dtype=np.int32), 1)
    zero_s5 = (jnp.zeros((bp, gp), F32), jnp.zeros((bp, gp), F32))
    zero_ml = (jnp.zeros((bp, ML_HEADS, ML_HD, ML_HD), F32), jnp.zeros((bp, ML_HEADS, ML_HD), F32),
               jnp.zeros((bp, ML_HEADS), F32))
    zero_conv = jnp.zeros((bp, ML_CONV - 1, 2 * qw), F32)
    xp = x_prompt.reshape(bp * tp, d)
    mem2 = mem_prompt.reshape(bp * n_mem, d)
    p_states = []
    for l, w in enumerate(layers):
        mkv = _linear([mem2], [w["ca_wkv"]], gamma=w["ca_mem_g"], tm=512)
        wd = mkv.shape[1] // 2
        mk = mkv[:, :wd].reshape(bp, n_mem, wd)
        mv = mkv[:, wd:].reshape(bp, n_mem, wd)
        xp, st = _layer(xp, bp, tp, w, tab_p, mk, mv, zero_s5, zero_ml, zero_conv, None,
                        final_norm_g if l == depth - 1 else None, cfg_p)
        p_states.append(st + (mk.reshape(bp, n_mem, CA_HEADS, CA_HD), mv.reshape(bp, n_mem, CA_HEADS, CA_HD)))
    y_prompt = xp.reshape(bp, tp, d)
    p_out = [jnp.stack([s[i] for s in p_states]) for i in range(10)]

    cfg_s = dict(tm=512, s5_tc=ts, ml_chunk=ts, seqs_per_step=8, tq=ts, ca_tt=ts, pages=32, kv_t=True)
    tab_s = _rope_tables(past_len + np.arange(ts, dtype=np.int32), bs)
    xs = x_sample.reshape(bs * ts, d)
    cache_rope_t = jnp.swapaxes(cache_k_rope, 2, 3)
    mem_k_t = jnp.transpose(cache_mem_k, (0, 1, 3, 4, 2)).reshape(depth, bs, -1, n_mem)
    mem_v_t = jnp.transpose(cache_mem_v, (0, 1, 3, 4, 2)).reshape(depth, bs, -1, n_mem)
    s_states = []
    for l, w in enumerate(layers):
        cfg_s["mem_layer"] = l
        xs, st = _layer(xs, bs, ts, w, tab_s, mem_k_t, mem_v_t,
                        (state_ssm_re[l].reshape(bs, gp), state_ssm_im[l].reshape(bs, gp)),
                        (state_mlstm_C[l], state_mlstm_n[l], state_mlstm_m[l]), state_mlstm_conv[l],
                        (cache_kv_latent, cache_rope_t, l, page_table),
                        final_norm_g if l == depth - 1 else None, cfg_s)
        s_states.append(st)
    y_sample = xs.reshape(bs, ts, d)
    s_out = [jnp.stack([s[i] for s in s_states]) for i in range(8)]

    return (y_prompt, y_sample, *p_out, *s_out)
```

```python
import functools
import math

import jax
import jax.numpy as jnp
import numpy as np
from jax import lax
from jax.experimental import pallas as pl
from jax.experimental.pallas import tpu as pltpu

F32 = jnp.float32
BF16 = jnp.bfloat16
NORM_EPS = 1e-6
ROPE_BASE = 10000.0

S5_CH = 16
S5_STATE = 64
ML_HEADS = 4
ML_HD = 64
ML_CONV = 4
MLA_HEADS = 8
MLA_NOPE = 64
MLA_ROPE = 32
MLA_V = 64
CA_HEADS = 4
CA_HD = 64
QK_PAD = 256

COL_U, COL_Q, COL_K, COL_V, COL_O, COL_CQ, COL_CKV, COL_MISC = 0, 256, 512, 768, 1024, 1280, 1536, 1664
N_IN_PAD = 1792
MISC_KR, MISC_IG, MISC_FG = 0, 32, 36

VMEM_LIMIT = 56 * 1024 * 1024


def _cparams(*sem):
    return pltpu.CompilerParams(dimension_semantics=sem, vmem_limit_bytes=VMEM_LIMIT)


def _rms(x, g):
    return x * lax.rsqrt(jnp.mean(x * x, axis=-1, keepdims=True) + NORM_EPS) * g


def _dot(a, b):
    return jnp.dot(a, b, preferred_element_type=F32)


def _dot_nt(a, b):
    return lax.dot_general(a, b, (((1,), (1,)), ((), ())), preferred_element_type=F32)


def _dot_tn(a, b):
    return lax.dot_general(a, b, (((0,), (0,)), ((), ())), preferred_element_type=F32)


def _const_spec(shape):
    nd = len(shape)
    return pl.BlockSpec(shape, lambda *_: (0,) * nd)


def _linear_body(*refs, n_in, has_norm, has_res, has_post):
    x_refs = refs[:n_in]
    pos = n_in
    g_ref = refs[pos] if has_norm else None
    pos += int(has_norm)
    w_refs = refs[pos:pos + n_in]
    pos += n_in
    res_ref = refs[pos] if has_res else None
    pos += int(has_res)
    post_refs = refs[pos:pos + 2] if has_post else None
    pos += 2 * int(has_post)
    o_ref = refs[pos]
    acc = None
    for x_ref, w_ref in zip(x_refs, w_refs):
        x = x_ref[...]
        if has_norm:
            x = _rms(x, g_ref[...])
        p = _dot(x.astype(BF16), w_ref[...])
        acc = p if acc is None else acc + p
    if has_res:
        acc = acc + res_ref[...]
    o_ref[...] = acc
    if has_post:
        refs[pos + 1][...] = _dot(_rms(acc, post_refs[0][...]).astype(BF16), post_refs[1][...]).astype(BF16)


def _linear(xs, ws, gamma=None, residual=None, post=None, tm=512):
    m = xs[0].shape[0]
    n = ws[0].shape[1]
    tm = min(tm, m)
    assert m % tm == 0
    in_specs = [pl.BlockSpec((tm, x.shape[1]), lambda i: (i, 0)) for x in xs]
    args = list(xs)
    if gamma is not None:
        in_specs.append(_const_spec((1, xs[0].shape[1])))
        args.append(gamma.reshape(1, -1))
    for w in ws:
        in_specs.append(_const_spec(w.shape))
        args.append(w)
    if residual is not None:
        in_specs.append(pl.BlockSpec((tm, n), lambda i: (i, 0)))
        args.append(residual)
    out_specs = [pl.BlockSpec((tm, n), lambda i: (i, 0))]
    out_shape = [jax.ShapeDtypeStruct((m, n), F32)]
    if post is not None:
        in_specs += [_const_spec((1, n)), _const_spec(post[1].shape)]
        args += [post[0].reshape(1, -1), post[1]]
        out_specs.append(pl.BlockSpec((tm, post[1].shape[1]), lambda i: (i, 0)))
        out_shape.append(jax.ShapeDtypeStruct((m, post[1].shape[1]), BF16))
    body = functools.partial(_linear_body, n_in=len(xs), has_norm=gamma is not None, has_res=residual is not None,
                             has_post=post is not None)
    outs = pl.pallas_call(
        body,
        grid=(m // tm,),
        in_specs=in_specs,
        out_specs=out_specs,
        out_shape=out_shape,
        compiler_params=_cparams("parallel"),
        name="linear",
    )(*args)
    return outs if post is not None else outs[0]


def _mlp_body(x_ref, a_ref, wa_ref, g_ref, w1_ref, w2_ref, gf_ref, o_ref, *, fc, final_norm):
    x = x_ref[...] + _dot(a_ref[...].astype(BF16), wa_ref[...])
    h = _rms(x, g_ref[...]).astype(BF16)
    acc = x
    for c in range(w1_ref.shape[1] // fc):
        a = _dot(h, w1_ref[:, c * fc:(c + 1) * fc])
        a = jnp.maximum(a, 0.0)
        acc = acc + _dot((a * a).astype(BF16), w2_ref[c * fc:(c + 1) * fc, :])
    if final_norm:
        acc = _rms(acc, gf_ref[...])
    o_ref[...] = acc


def _mlp(x, a, wa, gamma, w1, w2, final_gamma, final_norm, tm=512, fc=1024):
    m, d = x.shape
    dff = w1.shape[1]
    tm = min(tm, m)
    fc = min(fc, dff)
    body = functools.partial(_mlp_body, fc=fc, final_norm=final_norm)
    return pl.pallas_call(
        body,
        grid=(m // tm,),
        in_specs=[
            pl.BlockSpec((tm, d), lambda i: (i, 0)),
            pl.BlockSpec((tm, a.shape[1]), lambda i: (i, 0)),
            _const_spec(wa.shape),
            _const_spec((1, d)),
            pl.BlockSpec((d, dff), lambda i: (0, 0), pipeline_mode=pl.Buffered(1)),
            pl.BlockSpec((dff, d), lambda i: (0, 0), pipeline_mode=pl.Buffered(1)),
            _const_spec((1, d)),
        ],
        out_specs=pl.BlockSpec((tm, d), lambda i: (i, 0)),
        out_shape=jax.ShapeDtypeStruct((m, d), F32),
        compiler_params=_cparams("parallel"),
        name="mlp",
    )(x, a, wa, gamma.reshape(1, -1), w1, w2, final_gamma.reshape(1, -1))


def _s5_body(u_ref, h0r_ref, h0i_ref, are_ref, aim_ref, ldt_ref, br_ref, bi_ref, cr_ref, ci_ref, d_ref, wg_ref,
             go_ref, y_ref, hro_ref, hio_ref, xr_s, xi_s, hr_s, hi_s, *, tc, bb):
    @pl.when(pl.program_id(1) == 0)
    def _():
        hr_s[...] = h0r_ref[...]
        hi_s[...] = h0i_ref[...]

    ar = are_ref[...]
    ai = aim_ref[...]
    dt = jnp.exp(ldt_ref[...])
    mag = jnp.exp(ar * dt)
    lr = mag * jnp.cos(ai * dt)
    li = mag * jnp.sin(ai * dt)
    den = ar * ar + ai * ai
    zr = lr - 1.0
    fr = (zr * ar + li * ai) / den
    fi = (li * ar - zr * ai) / den

    ch = u_ref.shape[2]
    u = jnp.swapaxes(u_ref[...], 0, 1).reshape(tc * bb, ch)
    ub = u.astype(BF16)
    pr = _dot(ub, br_ref[...])
    pi = _dot(ub, bi_ref[...])
    xr_s[...] = fr * pr - fi * pi
    xi_s[...] = fr * pi + fi * pr

    def step(t, carry):
        hr, hi = carry
        rows = pl.ds(pl.multiple_of(t * bb, bb), bb)
        nr = lr * hr - li * hi + xr_s[rows, :]
        ni = lr * hi + li * hr + xi_s[rows, :]
        xr_s[rows, :] = nr
        xi_s[rows, :] = ni
        return nr, ni

    hr, hi = lax.fori_loop(0, tc, step, (hr_s[...], hi_s[...]))
    hr_s[...] = hr
    hi_s[...] = hi
    hro_ref[...] = hr
    hio_ref[...] = hi

    y = _dot(xr_s[...].astype(BF16), cr_ref[...]) - _dot(xi_s[...].astype(BF16), ci_ref[...])
    y = jax.nn.gelu(y + d_ref[...] * u)
    g = _dot(y.astype(BF16), wg_ref[...])
    o = g[:, :ch] * jax.nn.sigmoid(g[:, ch:])
    y_ref[...] = jnp.swapaxes(_rms(o, go_ref[...]).reshape(tc, bb, ch), 0, 1).astype(y_ref.dtype)


def _s5(z3, h0r, h0i, w, tc):
    b, t, _ = z3.shape
    ch = w["s5_d"].shape[1]
    gp = h0r.shape[1]
    bb = b if b <= 128 else 128
    tc = min(tc, t)
    body = functools.partial(_s5_body, tc=tc, bb=bb)
    row = lambda n: _const_spec((1, n))
    return pl.pallas_call(
        body,
        grid=(b // bb, t // tc),
        in_specs=[
            pl.BlockSpec((bb, tc, ch), lambda i, j: (i, j, COL_U // ch)),
            pl.BlockSpec((bb, gp), lambda i, j: (i, 0)),
            pl.BlockSpec((bb, gp), lambda i, j: (i, 0)),
            row(gp), row(gp), row(gp),
            _const_spec((ch, gp)), _const_spec((ch, gp)),
            _const_spec((gp, ch)), _const_spec((gp, ch)),
            row(ch), _const_spec((ch, 2 * ch)), row(ch),
        ],
        out_specs=[
            pl.BlockSpec((bb, tc, ch), lambda i, j: (i, j, 0)),
            pl.BlockSpec((bb, gp), lambda i, j: (i, 0)),
            pl.BlockSpec((bb, gp), lambda i, j: (i, 0)),
        ],
        out_shape=[
            jax.ShapeDtypeStruct((b, t, ch), BF16),
            jax.ShapeDtypeStruct((b, gp), F32),
            jax.ShapeDtypeStruct((b, gp), F32),
        ],
        scratch_shapes=[
            pltpu.VMEM((tc * bb, gp), F32), pltpu.VMEM((tc * bb, gp), F32),
            pltpu.VMEM((bb, gp), F32), pltpu.VMEM((bb, gp), F32),
        ],
        compiler_params=_cparams("parallel", "arbitrary"),
        name="s5",
    )(z3, h0r, h0i, w["s5_are"], w["s5_aim"], w["s5_ldt"], w["s5_br"], w["s5_bi"], w["s5_cr"], w["s5_ci"],
      w["s5_d"], w["s5_wglu"], w["s5_go"])


def _split3(x):
    hi = x.astype(BF16)
    r1 = x - hi.astype(F32)
    mid = r1.astype(BF16)
    lo = (r1 - mid.astype(F32)).astype(BF16)
    return hi, mid, lo


def _mlstm_seq(bi, zq_ref, zk_ref, zv_ref, zo_ref, misc_ref, conv0_ref, c0_ref, n0_ref, m0_ref, cw_ref, cb_ref,
               bcol_ref, gn_ref, psel_ref, y_ref, co_ref, no_ref, mo_ref, padq, padk, cs, ms, *, cl):
    hd = ML_HD
    qw = ML_HEADS * hd
    pw = 2 * hd
    blk = (lax.broadcasted_iota(jnp.int32, (pw, pw), 0) // hd) == (lax.broadcasted_iota(jnp.int32, (pw, pw), 1) // hd)
    blk2 = jnp.concatenate([blk, blk], axis=1)

    @pl.when(pl.program_id(1) == 0)
    def _():
        padq[bi, 0:8, :] = conv0_ref[bi][:, :qw]
        padk[bi, 0:8, :] = conv0_ref[bi][:, qw:]
        ms[bi] = m0_ref[bi]
        n_cols = jnp.concatenate([n0_ref[bi], jnp.zeros((pw - 8, pw), F32)], axis=0).T
        zero = jnp.zeros((hd, hd), F32)
        for j in range(ML_HEADS // 2):
            c_blk = jnp.concatenate([jnp.concatenate([c0_ref[bi, 2 * j], zero], axis=1),
                                     jnp.concatenate([zero, c0_ref[bi, 2 * j + 1]], axis=1)], axis=0)
            n_blk = jnp.where(blk, jnp.broadcast_to(n_cols[:, j:j + 1], (pw, pw)), 0.0)
            cs[bi, j] = jnp.concatenate([c_blk, n_blk], axis=1)

    lane = lax.broadcasted_iota(jnp.int32, (cl, pw), 1)
    first_half = lane < hd
    gcol = misc_ref[bi] + bcol_ref[...]
    ri = lax.broadcasted_iota(jnp.int32, (cl, cl), 0)
    ci = lax.broadcasted_iota(jnp.int32, (cl, cl), 1)
    tril = ri >= ci
    lower = jnp.where(tril, 1.0, 0.0).astype(BF16)
    bc_col = sum(_dot(lower, part) for part in _split3(jax.nn.log_sigmoid(gcol)))
    gate_src = jnp.where(lane < MISC_FG, gcol, bc_col)
    rep = sum(_dot(part, psel_ref[...]) for part in _split3(gate_src))
    gate_t = (gate_src if cl == pw else jnp.concatenate([gate_src, jnp.zeros((pw - cl, pw), F32)], axis=0)).T
    key_row = gate_t[MISC_IG:MISC_IG + ML_HEADS, :cl] - gate_t[MISC_FG:MISC_FG + ML_HEADS, :cl]

    yield
    padq[bi, 8:8 + cl, :] = zq_ref[bi]
    padk[bi, 8:8 + cl, :] = zk_ref[bi]
    cw = cw_ref[...]
    cb = cb_ref[...]

    def conv(pad, w, b):
        y = b
        for j in range(ML_CONV):
            y = y + pad[bi, 8 - (ML_CONV - 1) + j:8 - (ML_CONV - 1) + j + cl, :] * w[j:j + 1, :]
        return y

    q = jax.nn.silu(conv(padq, cw[:, :qw], cb[:, :qw]))
    k = jax.nn.silu(conv(padk, cw[:, qw:], cb[:, qw:])) * (hd ** -0.5)
    if cl >= 8:
        padq[bi, 0:8, :] = padq[bi, cl:cl + 8, :]
        padk[bi, 0:8, :] = padk[bi, cl:cl + 8, :]

    yield
    raw_scores, q_state = [], []
    for j in range(ML_HEADS // 2):
        ps = slice(j * pw, (j + 1) * pw)
        k2b = k[:, ps].astype(BF16)
        q_state.append(_dot(q[:, ps].astype(BF16), cs[bi, j].astype(BF16)))
        for a in range(2):
            mine = first_half if a == 0 else jnp.logical_not(first_half)
            raw_scores.append(_dot_nt(jnp.where(mine, q[:, ps], 0.0).astype(BF16), k2b))

    yield
    mean_blk = jnp.where(blk, 1.0 / hd, 0.0).astype(BF16)
    ones_slab = jnp.ones((cl, pw), BF16)
    gn = gn_ref[...]
    w_intra, w_inter, inv_floor, w_state, decays = [], [], [], [], []
    for h in range(ML_HEADS):
        ig_rep = rep[:, h * pw:(h + 1) * pw]
        bc_rep = rep[:, (ML_HEADS + h) * pw:(ML_HEADS + h + 1) * pw]
        m_prev = ms[bi, h:h + 1, :]
        d = jnp.where(tril, bc_rep[:, :cl] + key_row[h:h + 1, :], -jnp.inf)
        inter = bc_rep + m_prev
        m_tok = jnp.maximum(inter, jnp.max(d, axis=1, keepdims=True))
        w_intra.append(jnp.exp(d - m_tok[:, :cl]))
        w_inter.append(jnp.exp(inter - m_tok))
        inv_floor.append(jnp.exp(-m_tok))
        m_end = m_tok[cl - 1:cl, :]
        g_end = bc_rep[cl - 1:cl, :]
        w_state.append(jnp.exp(g_end - bc_rep + ig_rep - m_end))
        decays.append(jnp.exp(g_end + m_prev - m_end))
        ms[bi, h:h + 1, :] = m_end
    yield
    vo = [jnp.concatenate([zv_ref[bi][:, j * pw:(j + 1) * pw].astype(BF16), ones_slab], axis=1)
          for j in range(ML_HEADS // 2)]
    nds = [_dot((raw_scores[h] * w_intra[h]).astype(BF16), vo[h // 2]) for h in range(ML_HEADS)]
    yield
    for j in range(ML_HEADS // 2):
        ps = slice(j * pw, (j + 1) * pw)
        c_pair = cs[bi, j]
        qc = q_state[j]
        hh = []
        for h in (2 * j, 2 * j + 1):
            num = nds[h][:, :pw] + w_inter[h] * qc[:, :pw]
            den = nds[h][:, pw:] + w_inter[h] * qc[:, pw:]
            hh.append(num / jnp.maximum(jnp.abs(den), inv_floor[h]))
        hh2 = jnp.where(first_half, hh[0], hh[1])
        w_s2 = jnp.where(first_half, w_state[2 * j], w_state[2 * j + 1])
        decay2 = jnp.where(first_half[0:1, :], decays[2 * j], decays[2 * j + 1])
        upd = _dot_tn((k[:, ps] * w_s2).astype(BF16), vo[j])
        cs[bi, j] = jnp.concatenate([decay2, decay2], axis=1) * c_pair + jnp.where(blk2, upd, 0.0)
        oh = jax.nn.sigmoid(zo_ref[bi][:, ps]) * hh2
        sq = oh * oh
        sq_hi = sq.astype(BF16)
        msq = _dot(sq_hi, mean_blk) + _dot((sq - sq_hi.astype(F32)).astype(BF16), mean_blk)
        y_ref[bi, :, ps] = (oh * lax.rsqrt(msq + NORM_EPS) * gn[:, ps]).astype(y_ref.dtype)

    @pl.when(pl.program_id(1) == pl.num_programs(1) - 1)
    def _():
        mo_ref[bi] = ms[bi]
        no_ref[bi] = jnp.zeros(no_ref.shape[1:], F32)
        for j in range(ML_HEADS // 2):
            st = cs[bi, j]
            for a in range(2):
                co_ref[bi, 2 * j + a] = st[a * hd:(a + 1) * hd, a * hd:(a + 1) * hd]
            n_t = st[:, pw:].T
            no_ref[bi, j:j + 1, :] = n_t[0:1, :] + n_t[hd:hd + 1, :]


def _mlstm_body(*refs, cl, bb):
    pending = [_mlstm_seq(bi, *refs, cl=cl) for bi in range(bb)]
    while pending:
        pending = [g for g in pending if next(g, "done") != "done"]


def _mlstm(z3, conv0, c0, n0, m0, w, cl, bb):
    b, t, _ = z3.shape
    qw = ML_HEADS * ML_HD
    cl = min(cl, t)
    assert cl <= 2 * ML_HD and b % bb == 0
    body = functools.partial(_mlstm_body, cl=cl, bb=bb)
    zspec = lambda col: pl.BlockSpec((bb, cl, qw), lambda i, j: (i, j, col // qw))
    c_spec = pl.BlockSpec((bb, ML_HEADS, ML_HD, ML_HD), lambda i, j: (i, 0, 0, 0))
    row_spec = pl.BlockSpec((bb, 8, 128), lambda i, j: (i, 0, 0))
    return pl.pallas_call(
        body,
        grid=(b // bb, t // cl),
        in_specs=[
            zspec(COL_Q), zspec(COL_K), zspec(COL_V), zspec(COL_O),
            pl.BlockSpec((bb, cl, 128), lambda i, j: (i, j, COL_MISC // 128)),
            pl.BlockSpec((bb, 8, 2 * qw), lambda i, j: (i, 0, 0)),
            c_spec, row_spec, row_spec,
            _const_spec((ML_CONV, 2 * qw)), _const_spec((1, 2 * qw)),
            _const_spec((1, 128)), _const_spec((1, qw)), _const_spec(w["ml_psel"].shape),
        ],
        out_specs=[pl.BlockSpec((bb, cl, qw), lambda i, j: (i, j, 0)), c_spec, row_spec, row_spec],
        out_shape=[
            jax.ShapeDtypeStruct((b, t, qw), BF16),
            jax.ShapeDtypeStruct((b, ML_HEADS, ML_HD, ML_HD), F32),
            jax.ShapeDtypeStruct((b, 8, 128), F32),
            jax.ShapeDtypeStruct((b, 8, 128), F32),
        ],
        scratch_shapes=[
            pltpu.VMEM((bb, cl + 8, qw), F32), pltpu.VMEM((bb, cl + 8, qw), F32),
            pltpu.VMEM((bb, ML_HEADS // 2, 2 * ML_HD, 4 * ML_HD), F32), pltpu.VMEM((bb, 8, 128), F32),
        ],
        compiler_params=_cparams("parallel", "arbitrary"),
        name="mlstm",
    )(z3, z3, z3, z3, z3, conv0, c0, n0, m0, w["ml_cw"], w["ml_cb"], w["ml_bcol"], w["ml_gn"], w["ml_psel"])


def _rope128(x, cos_t, sin_up, sin_dn):
    half = MLA_ROPE // 2
    return x * cos_t + pltpu.roll(x, half, 1) * sin_up + pltpu.roll(x, 128 - half, 1) * sin_dn


def _mla_prep_body(zcq_ref, zckv_ref, misc_ref, cos_ref, sup_ref, sdn_ref, gq_ref, wn_ref, wr_ref, wuk_ref, gkv_ref,
                   sel_ref, *out_refs, heads_major):
    cos_t, sup, sdn = cos_ref[...], sup_ref[...], sdn_ref[...]
    cq = _rms(zcq_ref[...], gq_ref[...]).astype(BF16)
    qn = _dot(cq, wn_ref[...])
    qrp = _dot(cq, wr_ref[...])
    ql = [_dot(qn[:, 128 * j:128 * (j + 1)].astype(BF16), wuk_ref[j]).astype(BF16) for j in range(wuk_ref.shape[0])]
    qr = [_rope128(qrp[:, 128 * j:128 * (j + 1)], cos_t, sup, sdn).astype(BF16) for j in range(qrp.shape[1] // 128)]
    if heads_major:
        q_ref, c_ref, kcat_ref, kr_ref = out_refs
        kvl = ql[0].shape[1] // 2
        qr_all = jnp.concatenate(qr, axis=1)
        for h in range(MLA_HEADS):
            q_ref[0, h, :, 0:kvl] = ql[h // 2][:, (h % 2) * kvl:(h % 2 + 1) * kvl]
            q_ref[0, h, :, kvl:QK_PAD] = _dot(qr_all, sel_ref[h]).astype(BF16)
    else:
        ql_ref, qr_ref, c_ref, kcat_ref, kr_ref = out_refs
        for j, v in enumerate(ql):
            ql_ref[:, 256 * j:256 * (j + 1)] = v
        for j, v in enumerate(qr):
            qr_ref[:, 128 * j:128 * (j + 1)] = v
    c = _rms(zckv_ref[...], gkv_ref[...])
    c_ref[...] = c
    krf = _rope128(misc_ref[...], cos_t, sup, sdn)
    kr_ref[...] = krf[:, MISC_KR:MISC_KR + MLA_ROPE]
    lane = lax.broadcasted_iota(jnp.int32, krf.shape, 1)
    kcat_ref[:, 0:128] = c.astype(BF16)
    kcat_ref[:, 128:256] = jnp.where(lane < MLA_ROPE, krf, 0.0).astype(BF16)


def _mla_prep(z, tables, b, t_len, w, tm, heads_major):
    m = z.shape[0]
    tm = min(tm, m, t_len) if heads_major else m
    nt = tables[0].shape[0] // tm
    tspec = pl.BlockSpec((tm, 128), lambda i: (i % nt, 0))
    kvl = 128
    if heads_major:
        q_specs = [pl.BlockSpec((1, MLA_HEADS, tm, QK_PAD), lambda i: (i // nt, 0, i % nt, 0))]
        q_shapes = [jax.ShapeDtypeStruct((b, MLA_HEADS, t_len, QK_PAD), BF16)]
    else:
        q_specs = [pl.BlockSpec((tm, MLA_HEADS * kvl), lambda i: (i, 0)),
                   pl.BlockSpec((tm, MLA_HEADS * MLA_ROPE), lambda i: (i, 0))]
        q_shapes = [jax.ShapeDtypeStruct((m, MLA_HEADS * kvl), BF16),
                    jax.ShapeDtypeStruct((m, MLA_HEADS * MLA_ROPE), BF16)]
    return pl.pallas_call(
        functools.partial(_mla_prep_body, heads_major=heads_major),
        grid=(m // tm,),
        in_specs=[
            pl.BlockSpec((tm, 256), lambda i: (i, COL_CQ // 256)),
            pl.BlockSpec((tm, kvl), lambda i: (i, COL_CKV // 128)),
            pl.BlockSpec((tm, 128), lambda i: (i, COL_MISC // 128)),
            tspec, tspec, tspec,
            _const_spec((1, 256)), _const_spec(w["mla_wn"].shape), _const_spec(w["mla_wr"].shape),
            _const_spec(w["mla_wuk"].shape), _const_spec((1, kvl)), _const_spec(w["mla_sel"].shape),
        ],
        out_specs=q_specs + [
            pl.BlockSpec((tm, kvl), lambda i: (i, 0)),
            pl.BlockSpec((tm, QK_PAD), lambda i: (i, 0)),
            pl.BlockSpec((tm, MLA_ROPE), lambda i: (i, 0)),
        ],
        out_shape=q_shapes + [
            jax.ShapeDtypeStruct((m, kvl), F32),
            jax.ShapeDtypeStruct((m, QK_PAD), BF16),
            jax.ShapeDtypeStruct((m, MLA_ROPE), F32),
        ],
        compiler_params=_cparams("parallel"),
        name="mla_prep",
    )(z, z, z, tables[0], tables[1], tables[2], w["mla_gq"], w["mla_wn"], w["mla_wr"], w["mla_wuk"], w["mla_gkv"],
      w["mla_sel"])


def _mla_out(o_heads, wuv_ref, g_ref):
    ys = []
    for j in range(MLA_HEADS // 2):
        pair = jnp.concatenate([o_heads[2 * j], o_heads[2 * j + 1]], axis=1).astype(BF16)
        ys.append(_dot(pair, wuv_ref[j]))
    return _rms(jnp.concatenate(ys, axis=1), g_ref[...])


def _mla_prompt_body(q_ref, k_ref, wuv_ref, g_ref, o_ref, m_s, l_s, acc_s, *, tq, scale):
    i = pl.program_id(1)
    kvl = acc_s.shape[1]
    scale_log2e = scale * math.log2(math.e)
    m_s[...] = jnp.full(m_s.shape, -jnp.inf, F32)
    l_s[...] = jnp.zeros(l_s.shape, F32)
    acc_s[...] = jnp.zeros(acc_s.shape, F32)

    def block(start, width, diag_off):
        kb = k_ref[0, pl.ds(start, width), :]
        vb = kb[:, :kvl]
        if diag_off is not None:
            causal = (lax.broadcasted_iota(jnp.int32, (tq, width), 1)
                      <= lax.broadcasted_iota(jnp.int32, (tq, width), 0) + diag_off)
        s_next = _dot_nt(q_ref[0, 0], kb)
        for h in range(MLA_HEADS):
            rows = slice(h * tq, (h + 1) * tq)
            s = s_next
            if h + 1 < MLA_HEADS:
                s_next = _dot_nt(q_ref[0, h + 1], kb)
            if diag_off is not None:
                s = jnp.where(causal, s, -jnp.inf)
            m_prev = m_s[rows, :]
            m_next = jnp.maximum(m_prev, jnp.max(s, axis=1, keepdims=True))
            alpha = jnp.exp2((m_prev - m_next) * scale_log2e)
            p = jnp.exp2((s - jnp.concatenate([m_next] * (width // kvl), axis=1)) * scale_log2e)
            p_lanes = p[:, 0:kvl]
            for c in range(1, width // kvl):
                p_lanes = p_lanes + p[:, c * kvl:(c + 1) * kvl]
            l_s[rows, :] = alpha * l_s[rows, :] + p_lanes
            acc_s[rows, :] = alpha * acc_s[rows, :] + _dot(p.astype(BF16), vb)
            m_s[rows, :] = m_next

    def full_block(j, carry):
        block(pl.multiple_of(j * tq, tq), tq, None)
        return carry

    lax.fori_loop(0, i, full_block, 0)
    block(pl.multiple_of(i * tq, tq), tq, 0)

    o = acc_s[...] / jnp.sum(l_s[...], axis=1, keepdims=True)
    o_ref[0] = _mla_out([o[h * tq:(h + 1) * tq] for h in range(MLA_HEADS)], wuv_ref, g_ref).astype(o_ref.dtype)


def _mla_prompt(q, kcat, w, tq):
    b, hn, t, qk = q.shape
    tq = min(tq, t)
    kvl = 128
    wo = MLA_HEADS * MLA_V
    body = functools.partial(_mla_prompt_body, tq=tq, scale=1.0 / math.sqrt(MLA_NOPE + MLA_ROPE))
    return pl.pallas_call(
        body,
        grid=(b, t // tq),
        in_specs=[
            pl.BlockSpec((1, hn, tq, qk), lambda i, j: (i, 0, j, 0)),
            pl.BlockSpec((1, t, qk), lambda i, j: (i, 0, 0)),
            _const_spec(w["mla_wuv"].shape), _const_spec((1, wo)),
        ],
        out_specs=pl.BlockSpec((1, tq, wo), lambda i, j: (i, j, 0)),
        out_shape=jax.ShapeDtypeStruct((b, t, wo), BF16),
        scratch_shapes=[pltpu.VMEM((hn * tq, kvl), F32), pltpu.VMEM((hn * tq, kvl), F32),
                        pltpu.VMEM((hn * tq, kvl), F32)],
        compiler_params=_cparams("parallel", "parallel"),
        name="mla_prompt",
    )(q, kcat, w["mla_wuv"], w["mla_go"])


def _mla_sample_body(pt_ref, q_ref, kn_ref, cc_hbm, cr_hbm, wuv_ref, g_ref, o_ref, cbuf, rbuf, sem, m_s, l_s, acc_s,
                     *, layer, group, n_groups, t_new, scale):
    b = pl.program_id(0)
    kvl = acc_s.shape[1]
    page = cbuf.shape[1] // group

    def group_copies(bi, g, slot):
        cps = []
        for i in range(group):
            pid = pt_ref[bi, g * group + i]
            cps.append(pltpu.make_async_copy(cc_hbm.at[layer, pid], cbuf.at[slot, pl.ds(i * page, page), :],
                                             sem.at[slot]))
            cps.append(pltpu.make_async_copy(cr_hbm.at[layer, pid], rbuf.at[slot, :, pl.ds(i * page, page)],
                                             sem.at[slot]))
        return cps

    def start_group(bi, g, slot):
        for cp in group_copies(bi, g, slot):
            cp.start()

    def wait_group(bi, g, slot):
        for cp in group_copies(bi, g, slot):
            cp.wait()

    @pl.when(b == 0)
    def _():
        start_group(0, 0, 0)

    m_s[...] = jnp.full(m_s.shape, -jnp.inf, F32)
    l_s[...] = jnp.zeros(l_s.shape, F32)
    acc_s[...] = jnp.zeros(acc_s.shape, F32)
    q = q_ref[0].astype(BF16)
    q_lat = q[:, 0:kvl]
    q_rope = q[:, kvl:kvl + MLA_ROPE]

    first = lax.rem(b * n_groups, 2)
    for g in range(n_groups):
        slot = lax.rem(first + g, 2)
        if g + 1 < n_groups:
            start_group(b, g + 1, 1 - slot)
        else:
            @pl.when(b + 1 < pl.num_programs(0))
            def _():
                start_group(b + 1, 0, 1 - slot)
        wait_group(b, g, slot)
        cb = cbuf[slot].astype(BF16)
        rb = rbuf[slot].astype(BF16)
        s = (_dot_nt(q_lat, cb) + _dot(q_rope, rb)) * scale
        m_old = m_s[...]
        m_new = jnp.maximum(m_old, jnp.max(s, axis=1, keepdims=True))
        alpha = jnp.exp(m_old - m_new)
        p = jnp.exp(s - m_new)
        l_s[...] = alpha * l_s[...] + jnp.sum(p, axis=1, keepdims=True)
        acc_s[...] = alpha * acc_s[...] + _dot(p.astype(BF16), cb)
        m_s[...] = m_new

    kn = kn_ref[0].astype(F32)
    qf = q.astype(F32)
    t_row = lax.broadcasted_iota(jnp.int32, (q.shape[0], 1), 0) % t_new
    s_new = []
    for t2 in range(t_new):
        st = jnp.sum(qf * kn[t2:t2 + 1, :], axis=1, keepdims=True) * scale
        s_new.append(jnp.where(t_row >= t2, st, -jnp.inf))
    m_o = m_s[...]
    m_n = m_o
    for st in s_new:
        m_n = jnp.maximum(m_n, st)
    al = jnp.exp(m_o - m_n)
    l = al * l_s[...]
    acc = al * acc_s[...]
    for t2, st in enumerate(s_new):
        pt = jnp.exp(st - m_n)
        l = l + pt
        acc = acc + pt * kn[t2:t2 + 1, 0:kvl]
    acc_s[...] = acc / l
    o_ref[0] = _mla_out([acc_s[h * t_new:(h + 1) * t_new, :] for h in range(MLA_HEADS)], wuv_ref,
                        g_ref).astype(o_ref.dtype)


def _mla_sample(q, knew, cache_c, cache_r, layer, page_table, w, pages):
    b, rows, qk = q.shape
    t_new = knew.shape[1]
    n_pages = page_table.shape[1]
    page, kvl = cache_c.shape[2], cache_c.shape[3]
    group = min(pages, n_pages)
    assert n_pages % group == 0
    wo = MLA_HEADS * MLA_V
    body = functools.partial(_mla_sample_body, layer=layer, group=group, n_groups=n_pages // group, t_new=t_new,
                             scale=1.0 / math.sqrt(MLA_NOPE + MLA_ROPE))
    grid_spec = pltpu.PrefetchScalarGridSpec(
        num_scalar_prefetch=1,
        grid=(b,),
        in_specs=[
            pl.BlockSpec((1, rows, qk), lambda bi, pt: (bi, 0, 0)),
            pl.BlockSpec((1, t_new, qk), lambda bi, pt: (bi, 0, 0)),
            pl.BlockSpec(memory_space=pl.ANY),
            pl.BlockSpec(memory_space=pl.ANY),
            pl.BlockSpec(w["mla_wuv"].shape, lambda bi, pt: (0, 0, 0)),
            pl.BlockSpec((1, wo), lambda bi, pt: (0, 0)),
        ],
        out_specs=pl.BlockSpec((1, t_new, wo), lambda bi, pt: (bi, 0, 0)),
        scratch_shapes=[
            pltpu.VMEM((2, group * page, kvl), F32),
            pltpu.VMEM((2, MLA_ROPE, group * page), F32),
            pltpu.SemaphoreType.DMA((2,)),
            pltpu.VMEM((rows, 1), F32), pltpu.VMEM((rows, 1), F32), pltpu.VMEM((rows, kvl), F32),
        ],
    )
    return pl.pallas_call(
        body,
        grid_spec=grid_spec,
        out_shape=jax.ShapeDtypeStruct((b, t_new, wo), BF16),
        compiler_params=_cparams("arbitrary"),
        name="mla_sample",
    )(page_table, q, knew, cache_c, cache_r, w["mla_wuv"], w["mla_go"])


def _cross_body(q_ref, k_ref, v_ref, o_ref, *, kv_t, bb):
    k4 = len(k_ref.shape) == 4
    lane = lax.broadcasted_iota(jnp.int32, q_ref.shape[1:], 1)
    sels = [(lane >= h * CA_HD) & (lane < (h + 1) * CA_HD) for h in range(CA_HEADS)]

    def one_sequence(bi):
        q = q_ref[bi].astype(F32)
        kb = (k_ref[0, bi] if k4 else k_ref[bi]).astype(BF16)
        vb = (v_ref[0, bi] if k4 else v_ref[bi]).astype(BF16)
        scores = []
        for sel in sels:
            qh = jnp.where(sel, q, 0.0).astype(BF16)
            scores.append((_dot(qh, kb) if kv_t else _dot_nt(qh, kb)) * (CA_HD ** -0.5))
        yield
        probs = []
        for s in scores:
            e = jnp.exp(s - jnp.max(s, axis=1, keepdims=True))
            probs.append((e / jnp.sum(e, axis=1, keepdims=True)).astype(BF16))
        yield
        out = jnp.zeros(q.shape, F32)
        for sel, p in zip(sels, probs):
            out = out + jnp.where(sel, _dot_nt(p, vb) if kv_t else _dot(p, vb), 0.0)
        o_ref[bi] = out.astype(o_ref.dtype)

    pending = [one_sequence(bi) for bi in range(bb)]
    while pending:
        pending = [g for g in pending if next(g, "done") != "done"]


def _cross(q3, mem_k, mem_v, tt, kv_t, layer=None, bb=1):
    b, t, wd = q3.shape
    nm = mem_k.shape[-1] if kv_t else mem_k.shape[-2]
    tt = min(tt, t)
    assert b % bb == 0
    kv_block = (bb, wd, nm) if kv_t else (bb, nm, wd)
    if layer is None:
        kv_spec = pl.BlockSpec(kv_block, lambda i, j: (i, 0, 0))
    else:
        kv_spec = pl.BlockSpec((1,) + kv_block, lambda i, j: (layer, i, 0, 0))
    return pl.pallas_call(
        functools.partial(_cross_body, kv_t=kv_t, bb=bb),
        grid=(b // bb, t // tt),
        in_specs=[pl.BlockSpec((bb, tt, wd), lambda i, j: (i, j, 0)), kv_spec, kv_spec],
        out_specs=pl.BlockSpec((bb, tt, wd), lambda i, j: (i, j, 0)),
        out_shape=jax.ShapeDtypeStruct((b, t, wd), BF16),
        compiler_params=_cparams("parallel", "parallel"),
        name="cross_attn",
    )(q3, mem_k, mem_v)


def _prep_layer(p):
    w = {}
    wi = p["w_in"]
    d = wi.shape[0]
    o_i = 4 * 256
    w["w_in"] = jnp.concatenate(
        [wi[:, :o_i], wi[:, o_i + 8:o_i + 8 + 256 + 256 + 128 + 32], wi[:, o_i:o_i + 8],
         jnp.zeros((d, N_IN_PAD - wi.shape[1]), wi.dtype)], axis=1).astype(BF16)
    w["norm_mix_g"] = p["norm_mix_g"]

    g, pn = p["s5_A_re"].shape
    eye = jnp.eye(g, dtype=F32)
    w["s5_are"] = p["s5_A_re"].reshape(1, g * pn)
    w["s5_aim"] = p["s5_A_im"].reshape(1, g * pn)
    w["s5_ldt"] = jnp.repeat(p["s5_log_dt"], pn).reshape(1, g * pn)
    w["s5_br"] = jnp.einsum("gpc,gh->gchp", p["s5_B_re"], eye).reshape(g * S5_CH, g * pn).astype(BF16)
    w["s5_bi"] = jnp.einsum("gpc,gh->gchp", p["s5_B_im"], eye).reshape(g * S5_CH, g * pn).astype(BF16)
    w["s5_cr"] = jnp.einsum("gcp,gh->gphc", p["s5_C_re"], eye).reshape(g * pn, g * S5_CH).astype(BF16)
    w["s5_ci"] = jnp.einsum("gcp,gh->gphc", p["s5_C_im"], eye).reshape(g * pn, g * S5_CH).astype(BF16)
    w["s5_d"] = p["s5_D"].reshape(1, -1)
    w["s5_wglu"] = p["s5_w_glu"].astype(BF16)
    w["s5_go"] = p["s5_out_g"].reshape(1, -1)

    w["ml_cw"] = p["ml_conv_w"]
    w["ml_cb"] = p["ml_conv_b"].reshape(1, -1)
    gate_b = jnp.concatenate([p["ml_b_i"], p["ml_b_f"]])
    w["ml_bcol"] = jnp.zeros((1, 128), F32).at[0, MISC_IG:MISC_IG + 2 * ML_HEADS].set(gate_b)
    w["ml_gn"] = p["ml_norm_g"].reshape(1, -1)
    slab_of_col = jnp.arange(2 * ML_HEADS * 128) // 128
    w["ml_psel"] = (jnp.arange(128)[:, None] == (MISC_IG + slab_of_col)[None, :]).astype(BF16)

    wuq = p["mla_w_uq"].reshape(-1, MLA_HEADS, MLA_NOPE + MLA_ROPE)
    w["mla_gq"] = p["mla_q_norm_g"].reshape(1, -1)
    w["mla_wn"] = wuq[:, :, :MLA_NOPE].reshape(wuq.shape[0], -1).astype(BF16)
    w["mla_wr"] = wuq[:, :, MLA_NOPE:].reshape(wuq.shape[0], -1).astype(BF16)
    wuk = p["mla_w_uk"]
    kvl = wuk.shape[0]
    wuk_t = jnp.transpose(wuk, (1, 2, 0)).reshape(MLA_HEADS // 2, 2, MLA_NOPE, kvl)
    eye2 = jnp.eye(2, dtype=F32)
    w["mla_wuk"] = jnp.einsum("jand,ab->janbd", wuk_t, eye2).reshape(MLA_HEADS // 2, 2 * MLA_NOPE, 2 * kvl).astype(BF16)
    w["mla_gkv"] = p["mla_kv_norm_g"].reshape(1, -1)
    src = jnp.arange(MLA_HEADS * MLA_ROPE)
    w["mla_sel"] = ((src[None, :, None] // MLA_ROPE == jnp.arange(MLA_HEADS)[:, None, None])
                    & (src[None, :, None] % MLA_ROPE == jnp.arange(128)[None, None, :])).astype(BF16)
    wuv4 = p["mla_w_uv"].reshape(kvl, MLA_HEADS // 2, 2, MLA_V)
    w["mla_wuv"] = jnp.einsum("cjav,ab->jacbv", wuv4, eye2).reshape(MLA_HEADS // 2, 2 * kvl, 2 * MLA_V).astype(BF16)
    w["mla_go"] = p["mla_out_g"].reshape(1, -1)

    wo = p["w_out"].astype(BF16)
    w["w_out"] = [wo[:256], wo[256:512], wo[512:]]
    w["norm_ca_g"] = p["norm_ca_g"]
    w["ca_mem_g"] = p["ca_mem_g"]
    w["ca_wq"] = p["ca_w_q"].astype(BF16)
    w["ca_wkv"] = jnp.concatenate([p["ca_w_k"], p["ca_w_v"]], axis=1).astype(BF16)
    w["ca_wo"] = p["ca_w_o"].astype(BF16)
    w["norm_ffn_g"] = p["norm_ffn_g"]
    w["ffn_w1"] = p["ffn_w1"].astype(BF16)
    w["ffn_w2"] = p["ffn_w2"].astype(BF16)
    return w


def _rope_tables(pos, reps):
    half = MLA_ROPE // 2
    inv = ROPE_BASE ** (-np.arange(half, dtype=np.float64) * 2.0 / MLA_ROPE)
    ang = pos.astype(np.float64)[:, None] * inv[None, :]
    cos, sin = np.cos(ang).astype(np.float32), np.sin(ang).astype(np.float32)
    zero = np.zeros_like(sin)
    n = 128 // MLA_ROPE
    tabs = (np.tile(np.concatenate([cos, cos], 1), (reps, n)),
            np.tile(np.concatenate([zero, sin], 1), (reps, n)),
            np.tile(np.concatenate([-sin, zero], 1), (reps, n)))
    return tuple(jnp.asarray(t) for t in tabs)


def _layer(x2, b, t, w, tables, mem_k, mem_v, s5_h0, ml_state, conv_buf, paged, final_g, cfg):
    m = b * t
    qw = ML_HEADS * ML_HD
    z = _linear([x2], [w["w_in"]], gamma=w["norm_mix_g"], tm=cfg["tm"])
    z3 = z.reshape(b, t, N_IN_PAD)

    y_s5, s_re, s_im = _s5(z3, s5_h0[0], s5_h0[1], w, cfg["s5_tc"])
    y_s5 = y_s5.reshape(m, -1)

    c0, n0, m0 = ml_state
    m0_b = jnp.zeros((b, 8, 128), F32).at[:, :ML_HEADS, :].set(jnp.broadcast_to(m0[:, :, None], (b, ML_HEADS, 128)))
    n0_b = jnp.zeros((b, 8, 2 * ML_HD), F32).at[:, :ML_HEADS // 2, :].set(n0.reshape(b, ML_HEADS // 2, 2 * ML_HD))
    conv0 = jnp.zeros((b, 8, 2 * qw), F32).at[:, 8 - (ML_CONV - 1):, :].set(conv_buf)
    y_ml, ml_c, n_out, m_out = _mlstm(z3, conv0, c0, n0_b, m0_b, w, cfg["ml_chunk"], cfg["seqs_per_step"])
    y_ml = y_ml.reshape(m, qw)
    conv_new = jnp.concatenate([conv_buf, z3[:, :, COL_Q:COL_Q + 2 * qw]], axis=1)[:, t:, :]
    ml_n = n_out[:, :ML_HEADS // 2, :].reshape(b, ML_HEADS, ML_HD)
    ml_m = m_out[:, :ML_HEADS, 0]

    if paged is None:
        q4, c_lat, kcat, k_rope = _mla_prep(z, tables, b, t, w, cfg["tm"], True)
        kvl = c_lat.shape[1]
        y_mla = _mla_prompt(q4, kcat.reshape(b, t, QK_PAD), w, cfg["tq"])
    else:
        ql, qr, c_lat, kcat, k_rope = _mla_prep(z, tables, b, t, w, cfg["tm"], False)
        kvl = c_lat.shape[1]
        q4 = jnp.concatenate([ql.reshape(b, t, MLA_HEADS, kvl), qr.reshape(b, t, MLA_HEADS, MLA_ROPE),
                              jnp.zeros((b, t, MLA_HEADS, QK_PAD - kvl - MLA_ROPE), BF16)], axis=-1)
        q4 = jnp.transpose(q4, (0, 2, 1, 3))
        cache_c, cache_r, layer, page_table = paged
        y_mla = _mla_sample(q4.reshape(b, MLA_HEADS * t, QK_PAD).astype(F32), kcat.reshape(b, t, QK_PAD), cache_c,
                            cache_r, layer, page_table, w, cfg["pages"])
    y_mla = y_mla.reshape(m, MLA_HEADS * MLA_V)

    x2, qc = _linear([y_s5, y_ml, y_mla], w["w_out"], residual=x2, post=(w["norm_ca_g"], w["ca_wq"]), tm=cfg["tm"])

    oc = _cross(qc.reshape(b, t, -1), mem_k, mem_v, cfg["ca_tt"], cfg["kv_t"], cfg["mem_layer"], cfg["seqs_per_step"])

    x2 = _mlp(x2, oc.reshape(m, -1), w["ca_wo"], w["norm_ffn_g"], w["ffn_w1"], w["ffn_w2"],
              w["norm_ffn_g"] if final_g is None else final_g, final_g is not None, tm=cfg["tm"])
    pn = S5_STATE
    states = (s_re.reshape(b, -1, pn), s_im.reshape(b, -1, pn), ml_c, ml_n, ml_m, conv_new,
              c_lat.reshape(b, t, kvl), k_rope.reshape(b, t, MLA_ROPE))
    return x2, states


def kernel(x_prompt, x_sample, state_ssm_re, state_ssm_im, state_mlstm_C, state_mlstm_n, state_mlstm_m,
           state_mlstm_conv, cache_kv_latent, cache_k_rope, cache_mem_k, cache_mem_v, page_table, mem_prompt,
           norm_mix_g, w_in, s5_A_re, s5_A_im, s5_log_dt, s5_B_re, s5_B_im, s5_C_re, s5_C_im, s5_D, s5_w_glu,
           s5_out_g, ml_conv_w, ml_conv_b, ml_b_i, ml_b_f, ml_norm_g, mla_q_norm_g, mla_w_uq, mla_kv_norm_g,
           mla_w_uk, mla_w_uv, mla_out_g, w_out, norm_ca_g, ca_mem_g, ca_w_q, ca_w_k, ca_w_v, ca_w_o,
           norm_ffn_g, ffn_w1, ffn_w2, final_norm_g):
    stacked = dict(norm_mix_g=norm_mix_g, w_in=w_in, s5_A_re=s5_A_re, s5_A_im=s5_A_im, s5_log_dt=s5_log_dt,
                   s5_B_re=s5_B_re, s5_B_im=s5_B_im, s5_C_re=s5_C_re, s5_C_im=s5_C_im, s5_D=s5_D,
                   s5_w_glu=s5_w_glu, s5_out_g=s5_out_g, ml_conv_w=ml_conv_w, ml_conv_b=ml_conv_b, ml_b_i=ml_b_i,
                   ml_b_f=ml_b_f, ml_norm_g=ml_norm_g, mla_q_norm_g=mla_q_norm_g, mla_w_uq=mla_w_uq,
                   mla_kv_norm_g=mla_kv_norm_g, mla_w_uk=mla_w_uk, mla_w_uv=mla_w_uv, mla_out_g=mla_out_g,
                   w_out=w_out, norm_ca_g=norm_ca_g, ca_mem_g=ca_mem_g, ca_w_q=ca_w_q, ca_w_k=ca_w_k,
                   ca_w_v=ca_w_v, ca_w_o=ca_w_o, norm_ffn_g=norm_ffn_g, ffn_w1=ffn_w1, ffn_w2=ffn_w2)
    depth = w_in.shape[0]
    layers = [_prep_layer({k: v[l] for k, v in stacked.items()}) for l in range(depth)]

    bp, tp, d = x_prompt.shape
    bs, ts, _ = x_sample.shape
    gp = s5_A_re.shape[1] * s5_A_re.shape[2]
    qw = ML_HEADS * ML_HD
    n_mem = mem_prompt.shape[1]
    past_len = page_table.shape[1] * cache_kv_latent.shape[2]

    cfg_p = dict(tm=512, s5_tc=64, ml_chunk=128, seqs_per_step=1, tq=256, ca_tt=512, pages=1, kv_t=False, mem_layer=None)
    tab_p = _rope_tables(np.arange(tp, dtype=np.int32), 1)
    zero_s5 = (jnp.zeros((bp, gp), F32), jnp.zeros((bp, gp), F32))
    zero_ml = (jnp.zeros((bp, ML_HEADS, ML_HD, ML_HD), F32), jnp.zeros((bp, ML_HEADS, ML_HD), F32),
               jnp.zeros((bp, ML_HEADS), F32))
    zero_conv = jnp.zeros((bp, ML_CONV - 1, 2 * qw), F32)
    xp = x_prompt.reshape(bp * tp, d)
    mem2 = mem_prompt.reshape(bp * n_mem, d)
    p_states = []
    for l, w in enumerate(layers):
        mkv = _linear([mem2], [w["ca_wkv"]], gamma=w["ca_mem_g"], tm=512)
        wd = mkv.shape[1] // 2
        mk = mkv[:, :wd].reshape(bp, n_mem, wd)
        mv = mkv[:, wd:].reshape(bp, n_mem, wd)
        xp, st = _layer(xp, bp, tp, w, tab_p, mk, mv, zero_s5, zero_ml, zero_conv, None,
                        final_norm_g if l == depth - 1 else None, cfg_p)
        p_states.append(st + (mk.reshape(bp, n_mem, CA_HEADS, CA_HD), mv.reshape(bp, n_mem, CA_HEADS, CA_HD)))
    y_prompt = xp.reshape(bp, tp, d)
    p_out = [jnp.stack([s[i] for s in p_states]) for i in range(10)]

    cfg_s = dict(tm=512, s5_tc=ts, ml_chunk=ts, seqs_per_step=8, tq=ts, ca_tt=ts, pages=32, kv_t=True)
    tab_s = _rope_tables(past_len + np.arange(ts, dtype=np.int32), bs)
    xs = x_sample.reshape(bs * ts, d)
    cache_rope_t = jnp.swapaxes(cache_k_rope, 2, 3)
    mem_k_t = jnp.transpose(cache_mem_k, (0, 1, 3, 4, 2)).reshape(depth, bs, -1, n_mem)
    mem_v_t = jnp.transpose(cache_mem_v, (0, 1, 3, 4, 2)).reshape(depth, bs, -1, n_mem)
    s_states = []
    for l, w in enumerate(layers):
        cfg_s["mem_layer"] = l
        xs, st = _layer(xs, bs, ts, w, tab_s, mem_k_t, mem_v_t,
                        (state_ssm_re[l].reshape(bs, gp), state_ssm_im[l].reshape(bs, gp)),
                        (state_mlstm_C[l], state_mlstm_n[l], state_mlstm_m[l]), state_mlstm_conv[l],
                        (cache_kv_latent, cache_rope_t, l, page_table),
                        final_norm_g if l == depth - 1 else None, cfg_s)
        s_states.append(st)
    y_sample = xs.reshape(bs, ts, d)
    s_out = [jnp.stack([s[i] for s in s_states]) for i in range(8)]

    return (y_prompt, y_sample, *p_out, *s_out)
```

```python
import functools
import math

import jax
import jax.numpy as jnp
import numpy as np
from jax import lax
from jax.experimental import pallas as pl
from jax.experimental.pallas import tpu as pltpu

F32 = jnp.float32
BF16 = jnp.bfloat16
NORM_EPS = 1e-6
ROPE_BASE = 10000.0

S5_CH = 16
S5_STATE = 64
ML_HEADS = 4
ML_HD = 64
ML_CONV = 4
MLA_HEADS = 8
MLA_NOPE = 64
MLA_ROPE = 32
MLA_V = 64
CA_HEADS = 4
CA_HD = 64
QK_PAD = 256

COL_U, COL_Q, COL_K, COL_V, COL_O, COL_CQ, COL_CKV, COL_MISC = 0, 256, 512, 768, 1024, 1280, 1536, 1664
N_IN_PAD = 1792
MISC_KR, MISC_IG, MISC_FG = 0, 32, 36

VMEM_LIMIT = 56 * 1024 * 1024


def _cparams(*sem):
    return pltpu.CompilerParams(dimension_semantics=sem, vmem_limit_bytes=VMEM_LIMIT)


def _rms(x, g):
    return x * lax.rsqrt(jnp.mean(x * x, axis=-1, keepdims=True) + NORM_EPS) * g


def _dot(a, b):
    return jnp.dot(a, b, preferred_element_type=F32)


def _dot_nt(a, b):
    return lax.dot_general(a, b, (((1,), (1,)), ((), ())), preferred_element_type=F32)


def _dot_tn(a, b):
    return lax.dot_general(a, b, (((0,), (0,)), ((), ())), preferred_element_type=F32)


def _const_spec(shape):
    nd = len(shape)
    return pl.BlockSpec(shape, lambda *_: (0,) * nd)


def _linear_body(*refs, n_in, has_norm, has_res, has_post):
    x_refs = refs[:n_in]
    pos = n_in
    g_ref = refs[pos] if has_norm else None
    pos += int(has_norm)
    w_refs = refs[pos:pos + n_in]
    pos += n_in
    res_ref = refs[pos] if has_res else None
    pos += int(has_res)
    post_refs = refs[pos:pos + 2] if has_post else None
    pos += 2 * int(has_post)
    o_ref = refs[pos]
    acc = None
    for x_ref, w_ref in zip(x_refs, w_refs):
        x = x_ref[...]
        if has_norm:
            x = _rms(x, g_ref[...])
        p = _dot(x.astype(BF16), w_ref[...])
        acc = p if acc is None else acc + p
    if has_res:
        acc = acc + res_ref[...]
    o_ref[...] = acc
    if has_post:
        refs[pos + 1][...] = _dot(_rms(acc, post_refs[0][...]).astype(BF16), post_refs[1][...]).astype(BF16)


def _linear(xs, ws, gamma=None, residual=None, post=None, tm=512):
    m = xs[0].shape[0]
    n = ws[0].shape[1]
    tm = min(tm, m)
    assert m % tm == 0
    in_specs = [pl.BlockSpec((tm, x.shape[1]), lambda i: (i, 0)) for x in xs]
    args = list(xs)
    if gamma is not None:
        in_specs.append(_const_spec((1, xs[0].shape[1])))
        args.append(gamma.reshape(1, -1))
    for w in ws:
        in_specs.append(_const_spec(w.shape))
        args.append(w)
    if residual is not None:
        in_specs.append(pl.BlockSpec((tm, n), lambda i: (i, 0)))
        args.append(residual)
    out_specs = [pl.BlockSpec((tm, n), lambda i: (i, 0))]
    out_shape = [jax.ShapeDtypeStruct((m, n), F32)]
    if post is not None:
        in_specs += [_const_spec((1, n)), _const_spec(post[1].shape)]
        args += [post[0].reshape(1, -1), post[1]]
        out_specs.append(pl.BlockSpec((tm, post[1].shape[1]), lambda i: (i, 0)))
        out_shape.append(jax.ShapeDtypeStruct((m, post[1].shape[1]), BF16))
    body = functools.partial(_linear_body, n_in=len(xs), has_norm=gamma is not None, has_res=residual is not None,
                             has_post=post is not None)
    outs = pl.pallas_call(
        body,
        grid=(m // tm,),
        in_specs=in_specs,
        out_specs=out_specs,
        out_shape=out_shape,
        compiler_params=_cparams("parallel"),
        name="linear",
    )(*args)
    return outs if post is not None else outs[0]


def _mlp_body(x_ref, a_ref, wa_ref, g_ref, w1_ref, w2_ref, gf_ref, o_ref, *, fc, final_norm):
    x = x_ref[...] + _dot(a_ref[...].astype(BF16), wa_ref[...])
    h = _rms(x, g_ref[...]).astype(BF16)
    acc = x
    for c in range(w1_ref.shape[1] // fc):
        a = _dot(h, w1_ref[:, c * fc:(c + 1) * fc])
        a = jnp.maximum(a, 0.0)
        acc = acc + _dot((a * a).astype(BF16), w2_ref[c * fc:(c + 1) * fc, :])
    if final_norm:
        acc = _rms(acc, gf_ref[...])
    o_ref[...] = acc


def _mlp(x, a, wa, gamma, w1, w2, final_gamma, final_norm, tm=512, fc=1024):
    m, d = x.shape
    dff = w1.shape[1]
    tm = min(tm, m)
    fc = min(fc, dff)
    body = functools.partial(_mlp_body, fc=fc, final_norm=final_norm)
    return pl.pallas_call(
        body,
        grid=(m // tm,),
        in_specs=[
            pl.BlockSpec((tm, d), lambda i: (i, 0)),
            pl.BlockSpec((tm, a.shape[1]), lambda i: (i, 0)),
            _const_spec(wa.shape),
            _const_spec((1, d)),
            pl.BlockSpec((d, dff), lambda i: (0, 0), pipeline_mode=pl.Buffered(1)),
            pl.BlockSpec((dff, d), lambda i: (0, 0), pipeline_mode=pl.Buffered(1)),
            _const_spec((1, d)),
        ],
        out_specs=pl.BlockSpec((tm, d), lambda i: (i, 0)),
        out_shape=jax.ShapeDtypeStruct((m, d), F32),
        compiler_params=_cparams("parallel"),
        name="mlp",
    )(x, a, wa, gamma.reshape(1, -1), w1, w2, final_gamma.reshape(1, -1))


def _s5_body(u_ref, h0r_ref, h0i_ref, are_ref, aim_ref, ldt_ref, br_ref, bi_ref, cr_ref, ci_ref, d_ref, wg_ref,
             go_ref, y_ref, hro_ref, hio_ref, xr_s, xi_s, hr_s, hi_s, *, tc, bb):
    @pl.when(pl.program_id(1) == 0)
    def _():
        hr_s[...] = h0r_ref[...]
        hi_s[...] = h0i_ref[...]

    ar = are_ref[...]
    ai = aim_ref[...]
    dt = jnp.exp(ldt_ref[...])
    mag = jnp.exp(ar * dt)
    lr = mag * jnp.cos(ai * dt)
    li = mag * jnp.sin(ai * dt)
    den = ar * ar + ai * ai
    zr = lr - 1.0
    fr = (zr * ar + li * ai) / den
    fi = (li * ar - zr * ai) / den

    ch = u_ref.shape[2]
    u = jnp.swapaxes(u_ref[...], 0, 1).reshape(tc * bb, ch)
    ub = u.astype(BF16)
    pr = _dot(ub, br_ref[...])
    pi = _dot(ub, bi_ref[...])
    xr_s[...] = fr * pr - fi * pi
    xi_s[...] = fr * pi + fi * pr

    def scan(t0, t1, hr, hi):
        for t in range(t0, t1):
            rows = slice(t * bb, (t + 1) * bb)
            hr, hi = lr * hr - li * hi + xr_s[rows, :], lr * hi + li * hr + xi_s[rows, :]
            xr_s[rows, :] = hr
            xi_s[rows, :] = hi
        return hr, hi

    def project(t0, t1):
        rows = slice(t0 * bb, t1 * bb)
        y = _dot(xr_s[rows, :].astype(BF16), cr_ref[...]) - _dot(xi_s[rows, :].astype(BF16), ci_ref[...])
        y = jax.nn.gelu(y + d_ref[...] * u[rows, :])
        g = _dot(y.astype(BF16), wg_ref[...])
        o = g[:, :ch] * jax.nn.sigmoid(g[:, ch:])
        return _rms(o, go_ref[...])

    cuts = [0, tc // 2, tc] if tc >= 2 else [0, tc]
    hr, hi = hr_s[...], hi_s[...]
    outs = []
    for t0, t1 in zip(cuts[:-1], cuts[1:]):
        hr, hi = scan(t0, t1, hr, hi)
        outs.append(project(t0, t1))
    hr_s[...] = hr
    hi_s[...] = hi
    hro_ref[...] = hr
    hio_ref[...] = hi
    o = outs[0] if len(outs) == 1 else jnp.concatenate(outs, axis=0)
    y_ref[...] = jnp.swapaxes(o.reshape(tc, bb, ch), 0, 1).astype(y_ref.dtype)


def _s5(z3, h0r, h0i, w, tc):
    b, t, _ = z3.shape
    ch = w["s5_d"].shape[1]
    gp = h0r.shape[1]
    bb = b if b <= 128 else 128
    tc = min(tc, t)
    body = functools.partial(_s5_body, tc=tc, bb=bb)
    row = lambda n: _const_spec((1, n))
    return pl.pallas_call(
        body,
        grid=(b // bb, t // tc),
        in_specs=[
            pl.BlockSpec((bb, tc, ch), lambda i, j: (i, j, COL_U // ch)),
            pl.BlockSpec((bb, gp), lambda i, j: (i, 0)),
            pl.BlockSpec((bb, gp), lambda i, j: (i, 0)),
            row(gp), row(gp), row(gp),
            _const_spec((ch, gp)), _const_spec((ch, gp)),
            _const_spec((gp, ch)), _const_spec((gp, ch)),
            row(ch), _const_spec((ch, 2 * ch)), row(ch),
        ],
        out_specs=[
            pl.BlockSpec((bb, tc, ch), lambda i, j: (i, j, 0)),
            pl.BlockSpec((bb, gp), lambda i, j: (i, 0)),
            pl.BlockSpec((bb, gp), lambda i, j: (i, 0)),
        ],
        out_shape=[
            jax.ShapeDtypeStruct((b, t, ch), BF16),
            jax.ShapeDtypeStruct((b, gp), F32),
            jax.ShapeDtypeStruct((b, gp), F32),
        ],
        scratch_shapes=[
            pltpu.VMEM((tc * bb, gp), F32), pltpu.VMEM((tc * bb, gp), F32),
            pltpu.VMEM((bb, gp), F32), pltpu.VMEM((bb, gp), F32),
        ],
        compiler_params=_cparams("parallel", "arbitrary"),
        name="s5",
    )(z3, h0r, h0i, w["s5_are"], w["s5_aim"], w["s5_ldt"], w["s5_br"], w["s5_bi"], w["s5_cr"], w["s5_ci"],
      w["s5_d"], w["s5_wglu"], w["s5_go"])


def _split3(x):
    hi = x.astype(BF16)
    r1 = x - hi.astype(F32)
    mid = r1.astype(BF16)
    lo = (r1 - mid.astype(F32)).astype(BF16)
    return hi, mid, lo


def _mlstm_seq(bi, zq_ref, zk_ref, zv_ref, zo_ref, misc_ref, conv0_ref, c0_ref, n0_ref, m0_ref, cw_ref, cb_ref,
               bcol_ref, gn_ref, psel_ref, y_ref, co_ref, no_ref, mo_ref, padq, padk, cs, ms, *, cl):
    hd = ML_HD
    qw = ML_HEADS * hd
    pw = 2 * hd
    blk = (lax.broadcasted_iota(jnp.int32, (pw, pw), 0) // hd) == (lax.broadcasted_iota(jnp.int32, (pw, pw), 1) // hd)
    blk2 = jnp.concatenate([blk, blk], axis=1)

    @pl.when(pl.program_id(1) == 0)
    def _():
        padq[bi, 0:8, :] = conv0_ref[bi][:, :qw]
        padk[bi, 0:8, :] = conv0_ref[bi][:, qw:]
        ms[bi] = m0_ref[bi]
        n_cols = jnp.concatenate([n0_ref[bi], jnp.zeros((pw - 8, pw), F32)], axis=0).T
        zero = jnp.zeros((hd, hd), F32)
        for j in range(ML_HEADS // 2):
            c_blk = jnp.concatenate([jnp.concatenate([c0_ref[bi, 2 * j], zero], axis=1),
                                     jnp.concatenate([zero, c0_ref[bi, 2 * j + 1]], axis=1)], axis=0)
            n_blk = jnp.where(blk, jnp.broadcast_to(n_cols[:, j:j + 1], (pw, pw)), 0.0)
            cs[bi, j] = jnp.concatenate([c_blk, n_blk], axis=1)

    lane = lax.broadcasted_iota(jnp.int32, (cl, pw), 1)
    first_half = lane < hd
    gcol = misc_ref[bi] + bcol_ref[...]
    ri = lax.broadcasted_iota(jnp.int32, (cl, cl), 0)
    ci = lax.broadcasted_iota(jnp.int32, (cl, cl), 1)
    tril = ri >= ci
    lower = jnp.where(tril, 1.0, 0.0).astype(BF16)
    bc_col = sum(_dot(lower, part) for part in _split3(jax.nn.log_sigmoid(gcol)))
    gate_src = jnp.where(lane < MISC_FG, gcol, bc_col)
    rep = sum(_dot(part, psel_ref[...]) for part in _split3(gate_src))
    gate_t = (gate_src if cl == pw else jnp.concatenate([gate_src, jnp.zeros((pw - cl, pw), F32)], axis=0)).T
    key_row = gate_t[MISC_IG:MISC_IG + ML_HEADS, :cl] - gate_t[MISC_FG:MISC_FG + ML_HEADS, :cl]

    yield
    padq[bi, 8:8 + cl, :] = zq_ref[bi]
    padk[bi, 8:8 + cl, :] = zk_ref[bi]
    cw = cw_ref[...]
    cb = cb_ref[...]

    def conv(pad, w, b):
        y = b
        for j in range(ML_CONV):
            y = y + pad[bi, 8 - (ML_CONV - 1) + j:8 - (ML_CONV - 1) + j + cl, :] * w[j:j + 1, :]
        return y

    q = jax.nn.silu(conv(padq, cw[:, :qw], cb[:, :qw]))
    k = jax.nn.silu(conv(padk, cw[:, qw:], cb[:, qw:])) * (hd ** -0.5)
    if cl >= 8:
        padq[bi, 0:8, :] = padq[bi, cl:cl + 8, :]
        padk[bi, 0:8, :] = padk[bi, cl:cl + 8, :]

    yield
    raw_scores, q_state = [], []
    for j in range(ML_HEADS // 2):
        ps = slice(j * pw, (j + 1) * pw)
        k2b = k[:, ps].astype(BF16)
        q_state.append(_dot(q[:, ps].astype(BF16), cs[bi, j].astype(BF16)))
        for a in range(2):
            mine = first_half if a == 0 else jnp.logical_not(first_half)
            raw_scores.append(_dot_nt(jnp.where(mine, q[:, ps], 0.0).astype(BF16), k2b))

    yield
    mean_blk = jnp.where(blk, 1.0 / hd, 0.0).astype(BF16)
    ones_slab = jnp.ones((cl, pw), BF16)
    gn = gn_ref[...]
    w_intra, w_inter, inv_floor, w_state, decays = [], [], [], [], []
    for h in range(ML_HEADS):
        ig_rep = rep[:, h * pw:(h + 1) * pw]
        bc_rep = rep[:, (ML_HEADS + h) * pw:(ML_HEADS + h + 1) * pw]
        m_prev = ms[bi, h:h + 1, :]
        d = jnp.where(tril, bc_rep[:, :cl] + key_row[h:h + 1, :], -jnp.inf)
        inter = bc_rep + m_prev
        m_tok = jnp.maximum(inter, jnp.max(d, axis=1, keepdims=True))
        w_intra.append(jnp.exp(d - m_tok[:, :cl]))
        w_inter.append(jnp.exp(inter - m_tok))
        inv_floor.append(jnp.exp(-m_tok))
        m_end = m_tok[cl - 1:cl, :]
        g_end = bc_rep[cl - 1:cl, :]
        w_state.append(jnp.exp(g_end - bc_rep + ig_rep - m_end))
        decays.append(jnp.exp(g_end + m_prev - m_end))
        ms[bi, h:h + 1, :] = m_end
    yield
    vo = [jnp.concatenate([zv_ref[bi][:, j * pw:(j + 1) * pw].astype(BF16), ones_slab], axis=1)
          for j in range(ML_HEADS // 2)]
    nds = [_dot((raw_scores[h] * w_intra[h]).astype(BF16), vo[h // 2]) for h in range(ML_HEADS)]
    yield
    for j in range(ML_HEADS // 2):
        ps = slice(j * pw, (j + 1) * pw)
        c_pair = cs[bi, j]
        qc = q_state[j]
        hh = []
        for h in (2 * j, 2 * j + 1):
            num = nds[h][:, :pw] + w_inter[h] * qc[:, :pw]
            den = nds[h][:, pw:] + w_inter[h] * qc[:, pw:]
            hh.append(num / jnp.maximum(jnp.abs(den), inv_floor[h]))
        hh2 = jnp.where(first_half, hh[0], hh[1])
        w_s2 = jnp.where(first_half, w_state[2 * j], w_state[2 * j + 1])
        decay2 = jnp.where(first_half[0:1, :], decays[2 * j], decays[2 * j + 1])
        upd = _dot_tn((k[:, ps] * w_s2).astype(BF16), vo[j])
        cs[bi, j] = jnp.concatenate([decay2, decay2], axis=1) * c_pair + jnp.where(blk2, upd, 0.0)
        oh = jax.nn.sigmoid(zo_ref[bi][:, ps]) * hh2
        sq = oh * oh
        sq_hi = sq.astype(BF16)
        msq = _dot(sq_hi, mean_blk) + _dot((sq - sq_hi.astype(F32)).astype(BF16), mean_blk)
        y_ref[bi, :, ps] = (oh * lax.rsqrt(msq + NORM_EPS) * gn[:, ps]).astype(y_ref.dtype)

    @pl.when(pl.program_id(1) == pl.num_programs(1) - 1)
    def _():
        mo_ref[bi] = ms[bi]
        no_ref[bi] = jnp.zeros(no_ref.shape[1:], F32)
        for j in range(ML_HEADS // 2):
            st = cs[bi, j]
            for a in range(2):
                co_ref[bi, 2 * j + a] = st[a * hd:(a + 1) * hd, a * hd:(a + 1) * hd]
            n_t = st[:, pw:].T
            no_ref[bi, j:j + 1, :] = n_t[0:1, :] + n_t[hd:hd + 1, :]


def _mlstm_body(*refs, cl, bb):
    pending = [_mlstm_seq(bi, *refs, cl=cl) for bi in range(bb)]
    while pending:
        pending = [g for g in pending if next(g, "done") != "done"]


def _mlstm(z3, conv0, c0, n0, m0, w, cl, bb):
    b, t, _ = z3.shape
    qw = ML_HEADS * ML_HD
    cl = min(cl, t)
    assert cl <= 2 * ML_HD and b % bb == 0
    body = functools.partial(_mlstm_body, cl=cl, bb=bb)
    zspec = lambda col: pl.BlockSpec((bb, cl, qw), lambda i, j: (i, j, col // qw))
    c_spec = pl.BlockSpec((bb, ML_HEADS, ML_HD, ML_HD), lambda i, j: (i, 0, 0, 0))
    row_spec = pl.BlockSpec((bb, 8, 128), lambda i, j: (i, 0, 0))
    return pl.pallas_call(
        body,
        grid=(b // bb, t // cl),
        in_specs=[
            zspec(COL_Q), zspec(COL_K), zspec(COL_V), zspec(COL_O),
            pl.BlockSpec((bb, cl, 128), lambda i, j: (i, j, COL_MISC // 128)),
            pl.BlockSpec((bb, 8, 2 * qw), lambda i, j: (i, 0, 0)),
            c_spec, row_spec, row_spec,
            _const_spec((ML_CONV, 2 * qw)), _const_spec((1, 2 * qw)),
            _const_spec((1, 128)), _const_spec((1, qw)), _const_spec(w["ml_psel"].shape),
        ],
        out_specs=[pl.BlockSpec((bb, cl, qw), lambda i, j: (i, j, 0)), c_spec, row_spec, row_spec],
        out_shape=[
            jax.ShapeDtypeStruct((b, t, qw), BF16),
            jax.ShapeDtypeStruct((b, ML_HEADS, ML_HD, ML_HD), F32),
            jax.ShapeDtypeStruct((b, 8, 128), F32),
            jax.ShapeDtypeStruct((b, 8, 128), F32),
        ],
        scratch_shapes=[
            pltpu.VMEM((bb, cl + 8, qw), F32), pltpu.VMEM((bb, cl + 8, qw), F32),
            pltpu.VMEM((bb, ML_HEADS // 2, 2 * ML_HD, 4 * ML_HD), F32), pltpu.VMEM((bb, 8, 128), F32),
        ],
        compiler_params=_cparams("parallel", "arbitrary"),
        name="mlstm",
    )(z3, z3, z3, z3, z3, conv0, c0, n0, m0, w["ml_cw"], w["ml_cb"], w["ml_bcol"], w["ml_gn"], w["ml_psel"])


def _rope128(x, cos_t, sin_up, sin_dn):
    half = MLA_ROPE // 2
    return x * cos_t + pltpu.roll(x, half, 1) * sin_up + pltpu.roll(x, 128 - half, 1) * sin_dn


def _mla_prep_body(zcq_ref, zckv_ref, misc_ref, cos_ref, sup_ref, sdn_ref, gq_ref, wn_ref, wr_ref, wuk_ref, gkv_ref,
                   sel_ref, *out_refs, heads_major):
    cos_t, sup, sdn = cos_ref[...], sup_ref[...], sdn_ref[...]
    cq = _rms(zcq_ref[...], gq_ref[...]).astype(BF16)
    qn = _dot(cq, wn_ref[...])
    qrp = _dot(cq, wr_ref[...])
    ql = [_dot(qn[:, 128 * j:128 * (j + 1)].astype(BF16), wuk_ref[j]).astype(BF16) for j in range(wuk_ref.shape[0])]
    qr = [_rope128(qrp[:, 128 * j:128 * (j + 1)], cos_t, sup, sdn).astype(BF16) for j in range(qrp.shape[1] // 128)]
    if heads_major:
        q_ref, c_ref, kcat_ref, kr_ref = out_refs
        kvl = ql[0].shape[1] // 2
        qr_all = jnp.concatenate(qr, axis=1)
        for h in range(MLA_HEADS):
            q_ref[0, h, :, 0:kvl] = ql[h // 2][:, (h % 2) * kvl:(h % 2 + 1) * kvl]
            q_ref[0, h, :, kvl:QK_PAD] = _dot(qr_all, sel_ref[h]).astype(BF16)
    else:
        ql_ref, qr_ref, c_ref, kcat_ref, kr_ref = out_refs
        for j, v in enumerate(ql):
            ql_ref[:, 256 * j:256 * (j + 1)] = v
        for j, v in enumerate(qr):
            qr_ref[:, 128 * j:128 * (j + 1)] = v
    c = _rms(zckv_ref[...], gkv_ref[...])
    c_ref[...] = c
    krf = _rope128(misc_ref[...], cos_t, sup, sdn)
    kr_ref[...] = krf[:, MISC_KR:MISC_KR + MLA_ROPE]
    lane = lax.broadcasted_iota(jnp.int32, krf.shape, 1)
    kcat_ref[:, 0:128] = c.astype(BF16)
    kcat_ref[:, 128:256] = jnp.where(lane < MLA_ROPE, krf, 0.0).astype(BF16)


def _mla_prep(z, tables, b, t_len, w, tm, heads_major):
    m = z.shape[0]
    tm = min(tm, m, t_len) if heads_major else m
    nt = tables[0].shape[0] // tm
    tspec = pl.BlockSpec((tm, 128), lambda i: (i % nt, 0))
    kvl = 128
    if heads_major:
        q_specs = [pl.BlockSpec((1, MLA_HEADS, tm, QK_PAD), lambda i: (i // nt, 0, i % nt, 0))]
        q_shapes = [jax.ShapeDtypeStruct((b, MLA_HEADS, t_len, QK_PAD), BF16)]
    else:
        q_specs = [pl.BlockSpec((tm, MLA_HEADS * kvl), lambda i: (i, 0)),
                   pl.BlockSpec((tm, MLA_HEADS * MLA_ROPE), lambda i: (i, 0))]
        q_shapes = [jax.ShapeDtypeStruct((m, MLA_HEADS * kvl), BF16),
                    jax.ShapeDtypeStruct((m, MLA_HEADS * MLA_ROPE), BF16)]
    return pl.pallas_call(
        functools.partial(_mla_prep_body, heads_major=heads_major),
        grid=(m // tm,),
        in_specs=[
            pl.BlockSpec((tm, 256), lambda i: (i, COL_CQ // 256)),
            pl.BlockSpec((tm, kvl), lambda i: (i, COL_CKV // 128)),
            pl.BlockSpec((tm, 128), lambda i: (i, COL_MISC // 128)),
            tspec, tspec, tspec,
            _const_spec((1, 256)), _const_spec(w["mla_wn"].shape), _const_spec(w["mla_wr"].shape),
            _const_spec(w["mla_wuk"].shape), _const_spec((1, kvl)), _const_spec(w["mla_sel"].shape),
        ],
        out_specs=q_specs + [
            pl.BlockSpec((tm, kvl), lambda i: (i, 0)),
            pl.BlockSpec((tm, QK_PAD), lambda i: (i, 0)),
            pl.BlockSpec((tm, MLA_ROPE), lambda i: (i, 0)),
        ],
        out_shape=q_shapes + [
            jax.ShapeDtypeStruct((m, kvl), F32),
            jax.ShapeDtypeStruct((m, QK_PAD), BF16),
            jax.ShapeDtypeStruct((m, MLA_ROPE), F32),
        ],
        compiler_params=_cparams("parallel"),
        name="mla_prep",
    )(z, z, z, tables[0], tables[1], tables[2], w["mla_gq"], w["mla_wn"], w["mla_wr"], w["mla_wuk"], w["mla_gkv"],
      w["mla_sel"])


def _mla_out(o_heads, wuv_ref, g_ref):
    ys = []
    for j in range(MLA_HEADS // 2):
        pair = jnp.concatenate([o_heads[2 * j], o_heads[2 * j + 1]], axis=1).astype(BF16)
        ys.append(_dot(pair, wuv_ref[j]))
    return _rms(jnp.concatenate(ys, axis=1), g_ref[...])


def _mla_prompt_body(q_ref, k_ref, wuv_ref, g_ref, o_ref, m_s, l_s, acc_s, *, tq, scale):
    i = pl.program_id(1)
    kvl = acc_s.shape[1]
    scale_log2e = scale * math.log2(math.e)
    m_s[...] = jnp.full(m_s.shape, -jnp.inf, F32)
    l_s[...] = jnp.zeros(l_s.shape, F32)
    acc_s[...] = jnp.zeros(acc_s.shape, F32)

    def block(start, width, diag_off):
        kb = k_ref[0, pl.ds(start, width), :]
        vb = kb[:, :kvl]
        if diag_off is not None:
            causal = (lax.broadcasted_iota(jnp.int32, (tq, width), 1)
                      <= lax.broadcasted_iota(jnp.int32, (tq, width), 0) + diag_off)
        s_next = _dot_nt(q_ref[0, 0], kb)
        for h in range(MLA_HEADS):
            rows = slice(h * tq, (h + 1) * tq)
            s = s_next
            if h + 1 < MLA_HEADS:
                s_next = _dot_nt(q_ref[0, h + 1], kb)
            if diag_off is not None:
                s = jnp.where(causal, s, -jnp.inf)
            m_prev = m_s[rows, :]
            m_next = jnp.maximum(m_prev, jnp.max(s, axis=1, keepdims=True))
            alpha = jnp.exp2((m_prev - m_next) * scale_log2e)
            p = jnp.exp2((s - jnp.concatenate([m_next] * (width // kvl), axis=1)) * scale_log2e)
            p_lanes = p[:, 0:kvl]
            for c in range(1, width // kvl):
                p_lanes = p_lanes + p[:, c * kvl:(c + 1) * kvl]
            l_s[rows, :] = alpha * l_s[rows, :] + p_lanes
            acc_s[rows, :] = alpha * acc_s[rows, :] + _dot(p.astype(BF16), vb)
            m_s[rows, :] = m_next

    def full_block(j, carry):
        block(pl.multiple_of(j * tq, tq), tq, None)
        return carry

    lax.fori_loop(0, i, full_block, 0)
    block(pl.multiple_of(i * tq, tq), tq, 0)

    o = acc_s[...] / jnp.sum(l_s[...], axis=1, keepdims=True)
    o_ref[0] = _mla_out([o[h * tq:(h + 1) * tq] for h in range(MLA_HEADS)], wuv_ref, g_ref).astype(o_ref.dtype)


def _mla_prompt(q, kcat, w, tq):
    b, hn, t, qk = q.shape
    tq = min(tq, t)
    kvl = 128
    wo = MLA_HEADS * MLA_V
    body = functools.partial(_mla_prompt_body, tq=tq, scale=1.0 / math.sqrt(MLA_NOPE + MLA_ROPE))
    return pl.pallas_call(
        body,
        grid=(b, t // tq),
        in_specs=[
            pl.BlockSpec((1, hn, tq, qk), lambda i, j: (i, 0, j, 0)),
            pl.BlockSpec((1, t, qk), lambda i, j: (i, 0, 0)),
            _const_spec(w["mla_wuv"].shape), _const_spec((1, wo)),
        ],
        out_specs=pl.BlockSpec((1, tq, wo), lambda i, j: (i, j, 0)),
        out_shape=jax.ShapeDtypeStruct((b, t, wo), BF16),
        scratch_shapes=[pltpu.VMEM((hn * tq, kvl), F32), pltpu.VMEM((hn * tq, kvl), F32),
                        pltpu.VMEM((hn * tq, kvl), F32)],
        compiler_params=_cparams("parallel", "parallel"),
        name="mla_prompt",
    )(q, kcat, w["mla_wuv"], w["mla_go"])


def _mla_sample_body(pt_ref, q_ref, kn_ref, cc_hbm, cr_hbm, wuv_ref, g_ref, o_ref, cbuf, rbuf, sem, m_s, l_s, acc_s,
                     *, layer, group, n_groups, t_new, scale):
    b = pl.program_id(0)
    kvl = acc_s.shape[1]
    page = cbuf.shape[1] // group

    def group_copies(bi, g, slot):
        cps = []
        for i in range(group):
            pid = pt_ref[bi, g * group + i]
            cps.append(pltpu.make_async_copy(cc_hbm.at[layer, pid], cbuf.at[slot, pl.ds(i * page, page), :],
                                             sem.at[slot]))
            cps.append(pltpu.make_async_copy(cr_hbm.at[layer, pid], rbuf.at[slot, :, pl.ds(i * page, page)],
                                             sem.at[slot]))
        return cps

    def start_group(bi, g, slot):
        for cp in group_copies(bi, g, slot):
            cp.start()

    def wait_group(bi, g, slot):
        for cp in group_copies(bi, g, slot):
            cp.wait()

    @pl.when(b == 0)
    def _():
        start_group(0, 0, 0)

    m_s[...] = jnp.full(m_s.shape, -jnp.inf, F32)
    l_s[...] = jnp.zeros(l_s.shape, F32)
    acc_s[...] = jnp.zeros(acc_s.shape, F32)
    q = q_ref[0].astype(BF16)
    q_lat = q[:, 0:kvl]
    q_rope = q[:, kvl:kvl + MLA_ROPE]

    first = lax.rem(b * n_groups, 2)
    for g in range(n_groups):
        slot = lax.rem(first + g, 2)
        if g + 1 < n_groups:
            start_group(b, g + 1, 1 - slot)
        else:
            @pl.when(b + 1 < pl.num_programs(0))
            def _():
                start_group(b + 1, 0, 1 - slot)
        wait_group(b, g, slot)
        cb = cbuf[slot].astype(BF16)
        rb = rbuf[slot].astype(BF16)
        s = (_dot_nt(q_lat, cb) + _dot(q_rope, rb)) * scale
        m_old = m_s[...]
        m_new = jnp.maximum(m_old, jnp.max(s, axis=1, keepdims=True))
        alpha = jnp.exp(m_old - m_new)
        p = jnp.exp(s - m_new)
        l_s[...] = alpha * l_s[...] + jnp.sum(p, axis=1, keepdims=True)
        acc_s[...] = alpha * acc_s[...] + _dot(p.astype(BF16), cb)
        m_s[...] = m_new

    kn = kn_ref[0].astype(F32)
    qf = q.astype(F32)
    t_row = lax.broadcasted_iota(jnp.int32, (q.shape[0], 1), 0) % t_new
    s_new = []
    for t2 in range(t_new):
        st = jnp.sum(qf * kn[t2:t2 + 1, :], axis=1, keepdims=True) * scale
        s_new.append(jnp.where(t_row >= t2, st, -jnp.inf))
    m_o = m_s[...]
    m_n = m_o
    for st in s_new:
        m_n = jnp.maximum(m_n, st)
    al = jnp.exp(m_o - m_n)
    l = al * l_s[...]
    acc = al * acc_s[...]
    for t2, st in enumerate(s_new):
        pt = jnp.exp(st - m_n)
        l = l + pt
        acc = acc + pt * kn[t2:t2 + 1, 0:kvl]
    acc_s[...] = acc / l
    o_ref[0] = _mla_out([acc_s[h * t_new:(h + 1) * t_new, :] for h in range(MLA_HEADS)], wuv_ref,
                        g_ref).astype(o_ref.dtype)


def _mla_sample(q, knew, cache_c, cache_r, layer, page_table, w, pages):
    b, rows, qk = q.shape
    t_new = knew.shape[1]
    n_pages = page_table.shape[1]
    page, kvl = cache_c.shape[2], cache_c.shape[3]
    group = min(pages, n_pages)
    assert n_pages % group == 0
    wo = MLA_HEADS * MLA_V
    body = functools.partial(_mla_sample_body, layer=layer, group=group, n_groups=n_pages // group, t_new=t_new,
                             scale=1.0 / math.sqrt(MLA_NOPE + MLA_ROPE))
    grid_spec = pltpu.PrefetchScalarGridSpec(
        num_scalar_prefetch=1,
        grid=(b,),
        in_specs=[
            pl.BlockSpec((1, rows, qk), lambda bi, pt: (bi, 0, 0)),
            pl.BlockSpec((1, t_new, qk), lambda bi, pt: (bi, 0, 0)),
            pl.BlockSpec(memory_space=pl.ANY),
            pl.BlockSpec(memory_space=pl.ANY),
            pl.BlockSpec(w["mla_wuv"].shape, lambda bi, pt: (0, 0, 0)),
            pl.BlockSpec((1, wo), lambda bi, pt: (0, 0)),
        ],
        out_specs=pl.BlockSpec((1, t_new, wo), lambda bi, pt: (bi, 0, 0)),
        scratch_shapes=[
            pltpu.VMEM((2, group * page, kvl), F32),
            pltpu.VMEM((2, MLA_ROPE, group * page), F32),
            pltpu.SemaphoreType.DMA((2,)),
            pltpu.VMEM((rows, 1), F32), pltpu.VMEM((rows, 1), F32), pltpu.VMEM((rows, kvl), F32),
        ],
    )
    return pl.pallas_call(
        body,
        grid_spec=grid_spec,
        out_shape=jax.ShapeDtypeStruct((b, t_new, wo), BF16),
        compiler_params=_cparams("arbitrary"),
        name="mla_sample",
    )(page_table, q, knew, cache_c, cache_r, w["mla_wuv"], w["mla_go"])


def _cross_body(q_ref, k_ref, v_ref, o_ref, *, kv_t, bb):
    k4 = len(k_ref.shape) == 4
    lane = lax.broadcasted_iota(jnp.int32, q_ref.shape[1:], 1)
    sels = [(lane >= h * CA_HD) & (lane < (h + 1) * CA_HD) for h in range(CA_HEADS)]

    def one_sequence(bi):
        q = q_ref[bi].astype(F32)
        kb = (k_ref[0, bi] if k4 else k_ref[bi]).astype(BF16)
        vb = (v_ref[0, bi] if k4 else v_ref[bi]).astype(BF16)
        scores = []
        for sel in sels:
            qh = jnp.where(sel, q, 0.0).astype(BF16)
            scores.append((_dot(qh, kb) if kv_t else _dot_nt(qh, kb)) * (CA_HD ** -0.5))
        yield
        probs = []
        for s in scores:
            e = jnp.exp(s - jnp.max(s, axis=1, keepdims=True))
            probs.append((e / jnp.sum(e, axis=1, keepdims=True)).astype(BF16))
        yield
        out = jnp.zeros(q.shape, F32)
        for sel, p in zip(sels, probs):
            out = out + jnp.where(sel, _dot_nt(p, vb) if kv_t else _dot(p, vb), 0.0)
        o_ref[bi] = out.astype(o_ref.dtype)

    pending = [one_sequence(bi) for bi in range(bb)]
    while pending:
        pending = [g for g in pending if next(g, "done") != "done"]


def _cross(q3, mem_k, mem_v, tt, kv_t, layer=None, bb=1):
    b, t, wd = q3.shape
    nm = mem_k.shape[-1] if kv_t else mem_k.shape[-2]
    tt = min(tt, t)
    assert b % bb == 0
    kv_block = (bb, wd, nm) if kv_t else (bb, nm, wd)
    if layer is None:
        kv_spec = pl.BlockSpec(kv_block, lambda i, j: (i, 0, 0))
    else:
        kv_spec = pl.BlockSpec((1,) + kv_block, lambda i, j: (layer, i, 0, 0))
    return pl.pallas_call(
        functools.partial(_cross_body, kv_t=kv_t, bb=bb),
        grid=(b // bb, t // tt),
        in_specs=[pl.BlockSpec((bb, tt, wd), lambda i, j: (i, j, 0)), kv_spec, kv_spec],
        out_specs=pl.BlockSpec((bb, tt, wd), lambda i, j: (i, j, 0)),
        out_shape=jax.ShapeDtypeStruct((b, t, wd), BF16),
        compiler_params=_cparams("parallel", "parallel"),
        name="cross_attn",
    )(q3, mem_k, mem_v)


def _prep_layer(p):
    w = {}
    wi = p["w_in"]
    d = wi.shape[0]
    o_i = 4 * 256
    w["w_in"] = jnp.concatenate(
        [wi[:, :o_i], wi[:, o_i + 8:o_i + 8 + 256 + 256 + 128 + 32], wi[:, o_i:o_i + 8],
         jnp.zeros((d, N_IN_PAD - wi.shape[1]), wi.dtype)], axis=1).astype(BF16)
    w["norm_mix_g"] = p["norm_mix_g"]

    g, pn = p["s5_A_re"].shape
    eye = jnp.eye(g, dtype=F32)
    w["s5_are"] = p["s5_A_re"].reshape(1, g * pn)
    w["s5_aim"] = p["s5_A_im"].reshape(1, g * pn)
    w["s5_ldt"] = jnp.repeat(p["s5_log_dt"], pn).reshape(1, g * pn)
    w["s5_br"] = jnp.einsum("gpc,gh->gchp", p["s5_B_re"], eye).reshape(g * S5_CH, g * pn).astype(BF16)
    w["s5_bi"] = jnp.einsum("gpc,gh->gchp", p["s5_B_im"], eye).reshape(g * S5_CH, g * pn).astype(BF16)
    w["s5_cr"] = jnp.einsum("gcp,gh->gphc", p["s5_C_re"], eye).reshape(g * pn, g * S5_CH).astype(BF16)
    w["s5_ci"] = jnp.einsum("gcp,gh->gphc", p["s5_C_im"], eye).reshape(g * pn, g * S5_CH).astype(BF16)
    w["s5_d"] = p["s5_D"].reshape(1, -1)
    w["s5_wglu"] = p["s5_w_glu"].astype(BF16)
    w["s5_go"] = p["s5_out_g"].reshape(1, -1)

    w["ml_cw"] = p["ml_conv_w"]
    w["ml_cb"] = p["ml_conv_b"].reshape(1, -1)
    gate_b = jnp.concatenate([p["ml_b_i"], p["ml_b_f"]])
    w["ml_bcol"] = jnp.zeros((1, 128), F32).at[0, MISC_IG:MISC_IG + 2 * ML_HEADS].set(gate_b)
    w["ml_gn"] = p["ml_norm_g"].reshape(1, -1)
    slab_of_col = jnp.arange(2 * ML_HEADS * 128) // 128
    w["ml_psel"] = (jnp.arange(128)[:, None] == (MISC_IG + slab_of_col)[None, :]).astype(BF16)

    wuq = p["mla_w_uq"].reshape(-1, MLA_HEADS, MLA_NOPE + MLA_ROPE)
    w["mla_gq"] = p["mla_q_norm_g"].reshape(1, -1)
    w["mla_wn"] = wuq[:, :, :MLA_NOPE].reshape(wuq.shape[0], -1).astype(BF16)
    w["mla_wr"] = wuq[:, :, MLA_NOPE:].reshape(wuq.shape[0], -1).astype(BF16)
    wuk = p["mla_w_uk"]
    kvl = wuk.shape[0]
    wuk_t = jnp.transpose(wuk, (1, 2, 0)).reshape(MLA_HEADS // 2, 2, MLA_NOPE, kvl)
    eye2 = jnp.eye(2, dtype=F32)
    w["mla_wuk"] = jnp.einsum("jand,ab->janbd", wuk_t, eye2).reshape(MLA_HEADS // 2, 2 * MLA_NOPE, 2 * kvl).astype(BF16)
    w["mla_gkv"] = p["mla_kv_norm_g"].reshape(1, -1)
    src = jnp.arange(MLA_HEADS * MLA_ROPE)
    w["mla_sel"] = ((src[None, :, None] // MLA_ROPE == jnp.arange(MLA_HEADS)[:, None, None])
                    & (src[None, :, None] % MLA_ROPE == jnp.arange(128)[None, None, :])).astype(BF16)
    wuv4 = p["mla_w_uv"].reshape(kvl, MLA_HEADS // 2, 2, MLA_V)
    w["mla_wuv"] = jnp.einsum("cjav,ab->jacbv", wuv4, eye2).reshape(MLA_HEADS // 2, 2 * kvl, 2 * MLA_V).astype(BF16)
    w["mla_go"] = p["mla_out_g"].reshape(1, -1)

    wo = p["w_out"].astype(BF16)
    w["w_out"] = [wo[:256], wo[256:512], wo[512:]]
    w["norm_ca_g"] = p["norm_ca_g"]
    w["ca_mem_g"] = p["ca_mem_g"]
    w["ca_wq"] = p["ca_w_q"].astype(BF16)
    w["ca_wkv"] = jnp.concatenate([p["ca_w_k"], p["ca_w_v"]], axis=1).astype(BF16)
    w["ca_wo"] = p["ca_w_o"].astype(BF16)
    w["norm_ffn_g"] = p["norm_ffn_g"]
    w["ffn_w1"] = p["ffn_w1"].astype(BF16)
    w["ffn_w2"] = p["ffn_w2"].astype(BF16)
    return w


def _rope_tables(pos, reps):
    half = MLA_ROPE // 2
    inv = ROPE_BASE ** (-np.arange(half, dtype=np.float64) * 2.0 / MLA_ROPE)
    ang = pos.astype(np.float64)[:, None] * inv[None, :]
    cos, sin = np.cos(ang).astype(np.float32), np.sin(ang).astype(np.float32)
    zero = np.zeros_like(sin)
    n = 128 // MLA_ROPE
    tabs = (np.tile(np.concatenate([cos, cos], 1), (reps, n)),
            np.tile(np.concatenate([zero, sin], 1), (reps, n)),
            np.tile(np.concatenate([-sin, zero], 1), (reps, n)))
    return tuple(jnp.asarray(t) for t in tabs)


def _layer(x2, b, t, w, tables, mem_k, mem_v, s5_h0, ml_state, conv_buf, paged, final_g, cfg):
    m = b * t
    qw = ML_HEADS * ML_HD
    z = _linear([x2], [w["w_in"]], gamma=w["norm_mix_g"], tm=cfg["tm"])
    z3 = z.reshape(b, t, N_IN_PAD)

    y_s5, s_re, s_im = _s5(z3, s5_h0[0], s5_h0[1], w, cfg["s5_tc"])
    y_s5 = y_s5.reshape(m, -1)

    c0, n0, m0 = ml_state
    m0_b = jnp.zeros((b, 8, 128), F32).at[:, :ML_HEADS, :].set(jnp.broadcast_to(m0[:, :, None], (b, ML_HEADS, 128)))
    n0_b = jnp.zeros((b, 8, 2 * ML_HD), F32).at[:, :ML_HEADS // 2, :].set(n0.reshape(b, ML_HEADS // 2, 2 * ML_HD))
    conv0 = jnp.zeros((b, 8, 2 * qw), F32).at[:, 8 - (ML_CONV - 1):, :].set(conv_buf)
    y_ml, ml_c, n_out, m_out = _mlstm(z3, conv0, c0, n0_b, m0_b, w, cfg["ml_chunk"], cfg["seqs_per_step"])
    y_ml = y_ml.reshape(m, qw)
    conv_new = jnp.concatenate([conv_buf, z3[:, :, COL_Q:COL_Q + 2 * qw]], axis=1)[:, t:, :]
    ml_n = n_out[:, :ML_HEADS // 2, :].reshape(b, ML_HEADS, ML_HD)
    ml_m = m_out[:, :ML_HEADS, 0]

    if paged is None:
        q4, c_lat, kcat, k_rope = _mla_prep(z, tables, b, t, w, cfg["tm"], True)
        kvl = c_lat.shape[1]
        y_mla = _mla_prompt(q4, kcat.reshape(b, t, QK_PAD), w, cfg["tq"])
    else:
        ql, qr, c_lat, kcat, k_rope = _mla_prep(z, tables, b, t, w, cfg["tm"], False)
        kvl = c_lat.shape[1]
        q4 = jnp.concatenate([ql.reshape(b, t, MLA_HEADS, kvl), qr.reshape(b, t, MLA_HEADS, MLA_ROPE),
                              jnp.zeros((b, t, MLA_HEADS, QK_PAD - kvl - MLA_ROPE), BF16)], axis=-1)
        q4 = jnp.transpose(q4, (0, 2, 1, 3))
        cache_c, cache_r, layer, page_table = paged
        y_mla = _mla_sample(q4.reshape(b, MLA_HEADS * t, QK_PAD).astype(F32), kcat.reshape(b, t, QK_PAD), cache_c,
                            cache_r, layer, page_table, w, cfg["pages"])
    y_mla = y_mla.reshape(m, MLA_HEADS * MLA_V)

    x2, qc = _linear([y_s5, y_ml, y_mla], w["w_out"], residual=x2, post=(w["norm_ca_g"], w["ca_wq"]), tm=cfg["tm"])

    oc = _cross(qc.reshape(b, t, -1), mem_k, mem_v, cfg["ca_tt"], cfg["kv_t"], cfg["mem_layer"], cfg["seqs_per_step"])

    x2 = _mlp(x2, oc.reshape(m, -1), w["ca_wo"], w["norm_ffn_g"], w["ffn_w1"], w["ffn_w2"],
              w["norm_ffn_g"] if final_g is None else final_g, final_g is not None, tm=cfg["tm"])
    pn = S5_STATE
    states = (s_re.reshape(b, -1, pn), s_im.reshape(b, -1, pn), ml_c, ml_n, ml_m, conv_new,
              c_lat.reshape(b, t, kvl), k_rope.reshape(b, t, MLA_ROPE))
    return x2, states


def kernel(x_prompt, x_sample, state_ssm_re, state_ssm_im, state_mlstm_C, state_mlstm_n, state_mlstm_m,
           state_mlstm_conv, cache_kv_latent, cache_k_rope, cache_mem_k, cache_mem_v, page_table, mem_prompt,
           norm_mix_g, w_in, s5_A_re, s5_A_im, s5_log_dt, s5_B_re, s5_B_im, s5_C_re, s5_C_im, s5_D, s5_w_glu,
           s5_out_g, ml_conv_w, ml_conv_b, ml_b_i, ml_b_f, ml_norm_g, mla_q_norm_g, mla_w_uq, mla_kv_norm_g,
           mla_w_uk, mla_w_uv, mla_out_g, w_out, norm_ca_g, ca_mem_g, ca_w_q, ca_w_k, ca_w_v, ca_w_o,
           norm_ffn_g, ffn_w1, ffn_w2, final_norm_g):
    stacked = dict(norm_mix_g=norm_mix_g, w_in=w_in, s5_A_re=s5_A_re, s5_A_im=s5_A_im, s5_log_dt=s5_log_dt,
                   s5_B_re=s5_B_re, s5_B_im=s5_B_im, s5_C_re=s5_C_re, s5_C_im=s5_C_im, s5_D=s5_D,
                   s5_w_glu=s5_w_glu, s5_out_g=s5_out_g, ml_conv_w=ml_conv_w, ml_conv_b=ml_conv_b, ml_b_i=ml_b_i,
                   ml_b_f=ml_b_f, ml_norm_g=ml_norm_g, mla_q_norm_g=mla_q_norm_g, mla_w_uq=mla_w_uq,
                   mla_kv_norm_g=mla_kv_norm_g, mla_w_uk=mla_w_uk, mla_w_uv=mla_w_uv, mla_out_g=mla_out_g,
                   w_out=w_out, norm_ca_g=norm_ca_g, ca_mem_g=ca_mem_g, ca_w_q=ca_w_q, ca_w_k=ca_w_k,
                   ca_w_v=ca_w_v, ca_w_o=ca_w_o, norm_ffn_g=norm_ffn_g, ffn_w1=ffn_w1, ffn_w2=ffn_w2)
    depth = w_in.shape[0]
    layers = [_prep_layer({k: v[l] for k, v in stacked.items()}) for l in range(depth)]

    bp, tp, d = x_prompt.shape
    bs, ts, _ = x_sample.shape
    gp = s5_A_re.shape[1] * s5_A_re.shape[2]
    qw = ML_HEADS * ML_HD
    n_mem = mem_prompt.shape[1]
    past_len = page_table.shape[1] * cache_kv_latent.shape[2]

    cfg_p = dict(tm=512, s5_tc=64, ml_chunk=128, seqs_per_step=1, tq=256, ca_tt=512, pages=1, kv_t=False, mem_layer=None)
    tab_p = _rope_tables(np.arange(tp, dtype=np.int32), 1)
    zero_s5 = (jnp.zeros((bp, gp), F32), jnp.zeros((bp, gp), F32))
    zero_ml = (jnp.zeros((bp, ML_HEADS, ML_HD, ML_HD), F32), jnp.zeros((bp, ML_HEADS, ML_HD), F32),
               jnp.zeros((bp, ML_HEADS), F32))
    zero_conv = jnp.zeros((bp, ML_CONV - 1, 2 * qw), F32)
    xp = x_prompt.reshape(bp * tp, d)
    mem2 = mem_prompt.reshape(bp * n_mem, d)
    p_states = []
    for l, w in enumerate(layers):
        mkv = _linear([mem2], [w["ca_wkv"]], gamma=w["ca_mem_g"], tm=512)
        wd = mkv.shape[1] // 2
        mk = mkv[:, :wd].reshape(bp, n_mem, wd)
        mv = mkv[:, wd:].reshape(bp, n_mem, wd)
        xp, st = _layer(xp, bp, tp, w, tab_p, mk, mv, zero_s5, zero_ml, zero_conv, None,
                        final_norm_g if l == depth - 1 else None, cfg_p)
        p_states.append(st + (mk.reshape(bp, n_mem, CA_HEADS, CA_HD), mv.reshape(bp, n_mem, CA_HEADS, CA_HD)))
    y_prompt = xp.reshape(bp, tp, d)
    p_out = [jnp.stack([s[i] for s in p_states]) for i in range(10)]

    cfg_s = dict(tm=512, s5_tc=ts, ml_chunk=ts, seqs_per_step=8, tq=ts, ca_tt=ts, pages=32, kv_t=True)
    tab_s = _rope_tables(past_len + np.arange(ts, dtype=np.int32), bs)
    xs = x_sample.reshape(bs * ts, d)
    cache_rope_t = jnp.swapaxes(cache_k_rope, 2, 3)
    mem_k_t = jnp.transpose(cache_mem_k, (0, 1, 3, 4, 2)).reshape(depth, bs, -1, n_mem)
    mem_v_t = jnp.transpose(cache_mem_v, (0, 1, 3, 4, 2)).reshape(depth, bs, -1, n_mem)
    s_states = []
    for l, w in enumerate(layers):
        cfg_s["mem_layer"] = l
        xs, st = _layer(xs, bs, ts, w, tab_s, mem_k_t, mem_v_t,
                        (state_ssm_re[l].reshape(bs, gp), state_ssm_im[l].reshape(bs, gp)),
                        (state_mlstm_C[l], state_mlstm_n[l], state_mlstm_m[l]), state_mlstm_conv[l],
                        (cache_kv_latent, cache_rope_t, l, page_table),
                        final_norm_g if l == depth - 1 else None, cfg_s)
        s_states.append(st)
    y_sample = xs.reshape(bs, ts, d)
    s_out = [jnp.stack([s[i] for s in s_states]) for i in range(8)]

    return (y_prompt, y_sample, *p_out, *s_out)
```
